```python
import math
import jax
import jax.numpy as jnp
from jax import lax
import numpy as np

D_MODEL = 1024
BATCH = 4
SEQ = 8192
DEPTH = 2

CTX_LEN = 256
GRID_W = 64

D_MIX = D_MODEL
HEAD_DIM = 64
N_Q_HEADS = 8
N_KV_HEADS = 2
GQA_GROUP = N_Q_HEADS // N_KV_HEADS
ATTN_W = N_Q_HEADS * HEAD_DIM
KV_W = N_KV_HEADS * HEAD_DIM
WINDOW = 128
BLOCK = 128
ROPE_BASE = 10000.0

LRU_W = D_MIX // 4
LRU_HEADS = 4
LRU_HEAD_W = LRU_W // LRU_HEADS
LRU_C = 8.0

S5_W = D_MIX // 4
S5_GROUP = 16
S5_GROUPS = S5_W // S5_GROUP
S5_STATE = 64

N_DIR = 2
IN_W = ATTN_W + 2 * KV_W + 2 * LRU_W + S5_W
D_FF = ((8 * D_MODEL // 3 + 127) // 128) * 128
NEG = -1e30
EPS = 1e-6

kernel_name = 'hymba_style_attn_rglru_s5_prefix_ctx'


def rms_norm(x, g):
    xf = x.astype(jnp.float32)
    y = xf * lax.rsqrt(jnp.mean(xf * xf, axis=-1, keepdims=True) + EPS)
    return (y * g.astype(jnp.float32)).astype(x.dtype)


def modulate(h, shift, scale):
    return h * (1 + scale) + shift


def depthwise_conv(x, w, b):
    k = w.shape[0]
    lo = (k - 1) // 2
    L = x.shape[1]
    xp = jnp.pad(x, ((0, 0), (lo, k - 1 - lo), (0, 0)))
    out = b
    for j in range(k):
        out = out + xp[:, j:j + L] * w[j]
    return out


def axial_rope(L):
    t = jnp.arange(L)
    row = (t // GRID_W).astype(jnp.float32)
    col = (t % GRID_W).astype(jnp.float32)
    n_freq = HEAD_DIM // 4
    inv = ROPE_BASE ** (-jnp.arange(n_freq, dtype=jnp.float32) / n_freq)
    ang = jnp.stack([row[:, None] * inv, col[:, None] * inv], axis=1)
    return jnp.cos(ang)[None, :, None], jnp.sin(ang)[None, :, None]


def apply_rope(x, cos, sin):
    B, L, H, Dh = x.shape
    xr = x.reshape(B, L, H, 2, 2, Dh // 4)
    x1, x2 = xr[..., 0, :], xr[..., 1, :]
    o = jnp.stack([x1 * cos - x2 * sin, x2 * cos + x1 * sin], axis=-2)
    return o.reshape(B, L, H, Dh).astype(x.dtype)


def local_attention(q, k, v, k_ctx, v_ctx, sink):
    B, L = q.shape[:2]
    C = k_ctx.shape[1]
    nb = L // BLOCK
    scale = HEAD_DIM ** -0.5
    kp = jnp.pad(k, ((0, 0), (BLOCK, BLOCK), (0, 0), (0, 0)))
    vp = jnp.pad(v, ((0, 0), (BLOCK, BLOCK), (0, 0), (0, 0)))
    qb = q.reshape(B, nb, BLOCK, N_KV_HEADS, GQA_GROUP, HEAD_DIM).transpose(1, 0, 2, 3, 4, 5)
    sink_l = jnp.broadcast_to(sink.astype(jnp.float32).reshape(1, N_KV_HEADS, GQA_GROUP, 1, 1),
                              (B, N_KV_HEADS, GQA_GROUP, BLOCK, 1))

    def block(args):
        i, qi = args
        start = i * BLOCK
        ki = lax.dynamic_slice_in_dim(kp, start, 3 * BLOCK, axis=1)
        vi = lax.dynamic_slice_in_dim(vp, start, 3 * BLOCK, axis=1)
        s_loc = jnp.einsum('bqhgd,bkhd->bhgqk', qi, ki).astype(jnp.float32) * scale
        qpos = start + jnp.arange(BLOCK)
        kpos = start - BLOCK + jnp.arange(3 * BLOCK)
        valid = ((jnp.abs(qpos[:, None] - kpos[None, :]) <= WINDOW)
                 & (kpos >= 0)[None, :] & (kpos < L)[None, :])
        s_loc = jnp.where(valid, s_loc, NEG)
        s_ctx = jnp.einsum('bqhgd,bkhd->bhgqk', qi, k_ctx).astype(jnp.float32) * scale
        p = jax.nn.softmax(jnp.concatenate([s_loc, s_ctx, sink_l], axis=-1), axis=-1).astype(v.dtype)
        return (jnp.einsum('bhgqk,bkhd->bqhgd', p[..., :3 * BLOCK], vi)
                + jnp.einsum('bhgqk,bkhd->bqhgd', p[..., 3 * BLOCK:3 * BLOCK + C], v_ctx))

    ob = lax.map(block, (jnp.arange(nb), qb))
    return ob.transpose(1, 0, 2, 3, 4, 5).reshape(B, L, ATTN_W)


def context_attention(q, k, v, sink):
    B, C = q.shape[:2]
    scale = HEAD_DIM ** -0.5
    qg = q.reshape(B, C, N_KV_HEADS, GQA_GROUP, HEAD_DIM)
    s = jnp.einsum('bqhgd,bkhd->bhgqk', qg, k).astype(jnp.float32) * scale
    sink_c = jnp.broadcast_to(sink.astype(jnp.float32).reshape(1, N_KV_HEADS, GQA_GROUP, 1, 1),
                              (B, N_KV_HEADS, GQA_GROUP, C, 1))
    p = jax.nn.softmax(jnp.concatenate([s, sink_c], axis=-1), axis=-1).astype(v.dtype)
    return jnp.einsum('bhgqk,bkhd->bqhgd', p[..., :C], v).reshape(B, C, ATTN_W)


def _lin_combine(e1, e2):
    a1, b1 = e1
    a2, b2 = e2
    return a1 * a2, a2 * b1 + b2


def linear_scan(a, b, reverse, h0=None):
    a_cum, h = lax.associative_scan(_lin_combine, (a, b), reverse=reverse, axis=1)
    if h0 is not None:
        h = h + a_cum * h0[:, None]
    return h


def _cplx_combine(e1, e2):
    ar1, ai1, br1, bi1 = e1
    ar2, ai2, br2, bi2 = e2
    return (ar2 * ar1 - ai2 * ai1, ar2 * ai1 + ai2 * ar1,
            ar2 * br1 - ai2 * bi1 + br2, ar2 * bi1 + ai2 * br1 + bi2)


def complex_scan(a_re, a_im, b_re, b_im, reverse, h0=None):
    L = b_re.shape[1]
    ar = jnp.broadcast_to(a_re[None, None], (1, L) + a_re.shape)
    ai = jnp.broadcast_to(a_im[None, None], (1, L) + a_im.shape)
    acr, aci, hr, hi = lax.associative_scan(_cplx_combine, (ar, ai, b_re, b_im), reverse=reverse, axis=1)
    if h0 is not None:
        h0r, h0i = h0[0][:, None], h0[1][:, None]
        hr = hr + acr * h0r - aci * h0i
        hi = hi + acr * h0i + aci * h0r
    return hr, hi


def lru_coeffs(xc, wa, ba, wx, bx, lam):
    B, L, _ = xc.shape
    xf = xc.astype(jnp.float32)
    xh = xf.reshape(B, L, LRU_HEADS, LRU_HEAD_W)
    r = jax.nn.sigmoid(jnp.einsum('blhi,hij->blhj', xh, wa.astype(jnp.float32)).reshape(B, L, LRU_W) + ba)
    gi = jax.nn.sigmoid(jnp.einsum('blhi,hij->blhj', xh, wx.astype(jnp.float32)).reshape(B, L, LRU_W) + bx)
    log_a = -LRU_C * r * jax.nn.softplus(-lam.astype(jnp.float32))
    return jnp.exp(log_a), jnp.sqrt(-jnp.expm1(2 * log_a)) * gi * xf


def rglru_mixer(x_lat, x_ctx, p, need_ctx):
    xl = depthwise_conv(x_lat, p['lru_conv_w'], p['lru_conv_b'])
    xc = depthwise_conv(x_ctx, p['lru_conv_w'], p['lru_conv_b'])
    out_l = jnp.zeros(xl.shape, jnp.float32)
    out_c = jnp.zeros(xc.shape, jnp.float32)
    for d in range(N_DIR):
        rev = d == 1
        gates = (p['lru_wa'][d], p['lru_ba'][d], p['lru_wx'][d], p['lru_bx'][d], p['lru_lambda'][d])
        ac, bc = lru_coeffs(xc, *gates)
        hc = linear_scan(ac, bc, rev)
        h_end = hc[:, 0] if rev else hc[:, -1]
        al, bl = lru_coeffs(xl, *gates)
        out_l = out_l + linear_scan(al, bl, rev, h_end)
        if need_ctx:
            out_c = out_c + hc
    return out_l.astype(x_lat.dtype), out_c.astype(x_ctx.dtype)


def s5_discretise(lam_re, lam_im, log_step, b_re, b_im):
    lr = jnp.minimum(lam_re.astype(jnp.float32), -1e-4)
    li = lam_im.astype(jnp.float32)
    dt = jnp.exp(log_step.astype(jnp.float32))[:, None]
    mag = jnp.exp(lr * dt)
    ab_re = mag * jnp.cos(li * dt)
    ab_im = mag * jnp.sin(li * dt)
    nr = ab_re - 1
    den = lr * lr + li * li
    cr = ((nr * lr + ab_im * li) / den)[..., None]
    ci = ((ab_im * lr - nr * li) / den)[..., None]
    br = b_re.astype(jnp.float32)
    bi = b_im.astype(jnp.float32)
    return ab_re, ab_im, cr * br - ci * bi, cr * bi + ci * br


def s5_mixer(u_lat, u_ctx, p, need_ctx):
    def drive(u, bb_re, bb_im):
        ug = u.astype(jnp.float32).reshape(u.shape[0], u.shape[1], S5_GROUPS, S5_GROUP)
        return jnp.einsum('blgc,gnc->blgn', ug, bb_re), jnp.einsum('blgc,gnc->blgn', ug, bb_im)

    def readout(hr, hi, c_re, c_im):
        y = (jnp.einsum('blgn,gcn->blgc', hr, c_re.astype(jnp.float32))
             - jnp.einsum('blgn,gcn->blgc', hi, c_im.astype(jnp.float32)))
        return y.reshape(hr.shape[0], hr.shape[1], S5_W)

    def glu(y):
        z = jax.nn.gelu(y)
        return z * jax.nn.sigmoid(z @ p['s5_w_glu'].astype(jnp.float32))

    dskip = p['s5_d'].astype(jnp.float32)
    y_l = u_lat.astype(jnp.float32) * dskip
    y_c = u_ctx.astype(jnp.float32) * dskip
    for d in range(N_DIR):
        rev = d == 1
        ab_re, ab_im, bb_re, bb_im = s5_discretise(p['s5_lam_re'][d], p['s5_lam_im'][d],
                                                   p['s5_log_step'][d], p['s5_b_re'][d], p['s5_b_im'][d])
        cr, ci = drive(u_ctx, bb_re, bb_im)
        hcr, hci = complex_scan(ab_re, ab_im, cr, ci, rev)
        idx = 0 if rev else -1
        h_end = (hcr[:, idx], hci[:, idx])
        lr_, li_ = drive(u_lat, bb_re, bb_im)
        hlr, hli = complex_scan(ab_re, ab_im, lr_, li_, rev, h_end)
        y_l = y_l + readout(hlr, hli, p['s5_c_re'][d], p['s5_c_im'][d])
        if need_ctx:
            y_c = y_c + readout(hcr, hci, p['s5_c_re'][d], p['s5_c_im'][d])
    out_l = glu(y_l).astype(u_lat.dtype)
    out_c = glu(y_c).astype(u_ctx.dtype) if need_ctx else None
    return out_l, out_c


def split_in(proj):
    cuts = np.cumsum([ATTN_W, KV_W, KV_W, LRU_W, LRU_W]).tolist()
    return jnp.split(proj, cuts, axis=-1)


def merge_groups(o_attn, o_lru, o_s5, mix_g, w_out):
    o = jnp.concatenate([rms_norm(o_attn, mix_g[:ATTN_W]),
                         rms_norm(o_lru, mix_g[ATTN_W:ATTN_W + LRU_W]),
                         rms_norm(o_s5, mix_g[ATTN_W + LRU_W:])], axis=-1)
    return o @ w_out


def heads(t, n):
    return t.reshape(t.shape[0], t.shape[1], n, HEAD_DIM)


def token_mixers(h_lat, h_ctx, p, rope, need_ctx):
    ql, kl, vl, gl, xl, ul = split_in(h_lat @ p['w_in'])
    qc, kc, vc, gc, xc, uc = split_in(h_ctx @ p['w_in'])
    cos, sin = rope
    ql = apply_rope(rms_norm(heads(ql, N_Q_HEADS), p['q_norm_g']), cos, sin)
    kl = apply_rope(rms_norm(heads(kl, N_KV_HEADS), p['k_norm_g']), cos, sin)
    kc = rms_norm(heads(kc, N_KV_HEADS), p['k_norm_g'])
    vl = heads(vl, N_KV_HEADS)
    vc = heads(vc, N_KV_HEADS)
    attn_l = local_attention(ql, kl, vl, kc, vc, p['attn_sink'])
    lru_l, lru_c = rglru_mixer(xl, xc, p, need_ctx)
    s5_l, s5_c = s5_mixer(ul, uc, p, need_ctx)
    out_l = merge_groups(attn_l, jax.nn.gelu(gl) * lru_l, s5_l, p['mix_g'], p['w_out'])
    out_c = None
    if need_ctx:
        qc = rms_norm(heads(qc, N_Q_HEADS), p['q_norm_g'])
        attn_c = context_attention(qc, kc, vc, p['attn_sink'])
        out_c = merge_groups(attn_c, jax.nn.gelu(gc) * lru_c, s5_c, p['mix_g'], p['w_out'])
    return out_l, out_c


def conv_glu_ffn(h, w_up, conv_w, conv_b, w_down):
    g, v = jnp.split(h @ w_up, 2, axis=-1)
    return (jax.nn.gelu(depthwise_conv(g, conv_w, conv_b)) * v) @ w_down


def setup_inputs(seed: int = 0) -> dict:
    key = jax.random.key(seed)
    ks = iter(jax.random.split(key, 48))
    f32 = jnp.float32

    def nrm(shape, scale):
        return scale * jax.random.normal(next(ks), shape, f32)

    def gain(shape):
        return 1.0 + nrm(shape, 0.02)

    u = jax.random.uniform(next(ks), (DEPTH, N_DIR, LRU_W), f32, 0.9, 0.999)
    s = u ** (1.0 / LRU_C)
    lru_lambda = jnp.log(s) - jnp.log1p(-s)
    n_idx = jnp.arange(S5_STATE, dtype=f32)
    s5_lam_re = -0.5 + nrm((DEPTH, N_DIR, S5_GROUPS, S5_STATE), 0.01)
    s5_lam_im = jnp.pi * n_idx + nrm((DEPTH, N_DIR, S5_GROUPS, S5_STATE), 0.01)
    s5_log_step = jax.random.uniform(next(ks), (DEPTH, N_DIR, S5_GROUPS), f32,
                                     math.log(1e-3), math.log(1e-1))
    return {
        'x': nrm((BATCH, SEQ, D_MODEL), 1.0),
        'c': nrm((BATCH, D_MODEL), 1.0),
        'ctx': nrm((BATCH, CTX_LEN, D_MODEL), 1.0),
        'c_ctx': nrm((D_MODEL,), 1.0),
        'w_mod': nrm((DEPTH, D_MODEL, 6 * D_MODEL), 0.5 * D_MODEL ** -0.5),
        'b_mod': nrm((DEPTH, 6 * D_MODEL), 0.02),
        'norm1_g': gain((DEPTH, D_MODEL)),
        'norm2_g': gain((DEPTH, D_MODEL)),
        'w_in': nrm((DEPTH, D_MODEL, IN_W), D_MODEL ** -0.5),
        'q_norm_g': gain((DEPTH, HEAD_DIM)),
        'k_norm_g': gain((DEPTH, HEAD_DIM)),
        'attn_sink': nrm((DEPTH, N_Q_HEADS), 0.5),
        'lru_conv_w': nrm((DEPTH, 4, LRU_W), 0.5),
        'lru_conv_b': nrm((DEPTH, LRU_W), 0.02),
        'lru_wa': nrm((DEPTH, N_DIR, LRU_HEADS, LRU_HEAD_W, LRU_HEAD_W), LRU_HEAD_W ** -0.5),
        'lru_ba': nrm((DEPTH, N_DIR, LRU_W), 0.02),
        'lru_wx': nrm((DEPTH, N_DIR, LRU_HEADS, LRU_HEAD_W, LRU_HEAD_W), LRU_HEAD_W ** -0.5),
        'lru_bx': nrm((DEPTH, N_DIR, LRU_W), 0.02),
        'lru_lambda': lru_lambda,
        's5_lam_re': s5_lam_re,
        's5_lam_im': s5_lam_im,
        's5_log_step': s5_log_step,
        's5_b_re': nrm((DEPTH, N_DIR, S5_GROUPS, S5_STATE, S5_GROUP), 0.5 * S5_GROUP ** -0.5),
        's5_b_im': nrm((DEPTH, N_DIR, S5_GROUPS, S5_STATE, S5_GROUP), 0.5 * S5_GROUP ** -0.5),
        's5_c_re': nrm((DEPTH, N_DIR, S5_GROUPS, S5_GROUP, S5_STATE), 0.5 * S5_STATE ** -0.5),
        's5_c_im': nrm((DEPTH, N_DIR, S5_GROUPS, S5_GROUP, S5_STATE), 0.5 * S5_STATE ** -0.5),
        's5_d': nrm((DEPTH, S5_W), 1.0),
        's5_w_glu': nrm((DEPTH, S5_W, S5_W), S5_W ** -0.5),
        'mix_g': gain((DEPTH, D_MIX)),
        'w_out': nrm((DEPTH, D_MIX, D_MODEL), D_MIX ** -0.5),
        'ffn_w_up': nrm((DEPTH, D_MODEL, 2 * D_FF), D_MODEL ** -0.5),
        'ffn_conv_w': nrm((DEPTH, 3, D_FF), 3 ** -0.5),
        'ffn_conv_b': nrm((DEPTH, D_FF), 0.02),
        'ffn_w_down': nrm((DEPTH, D_FF, D_MODEL), D_FF ** -0.5),
    }


def reference(x, c, ctx, c_ctx, w_mod, b_mod, norm1_g, norm2_g, w_in, q_norm_g, k_norm_g, attn_sink,
              lru_conv_w, lru_conv_b, lru_wa, lru_ba, lru_wx, lru_bx, lru_lambda,
              s5_lam_re, s5_lam_im, s5_log_step, s5_b_re, s5_b_im, s5_c_re, s5_c_im, s5_d, s5_w_glu,
              mix_g, w_out, ffn_w_up, ffn_conv_w, ffn_conv_b, ffn_w_down):
    L = x.shape[1]
    rope = axial_rope(L)
    for l in range(DEPTH):
        need_ctx = l < DEPTH - 1
        p = {
            'w_in': w_in[l], 'q_norm_g': q_norm_g[l], 'k_norm_g': k_norm_g[l], 'attn_sink': attn_sink[l],
            'lru_conv_w': lru_conv_w[l], 'lru_conv_b': lru_conv_b[l], 'lru_wa': lru_wa[l], 'lru_ba': lru_ba[l],
            'lru_wx': lru_wx[l], 'lru_bx': lru_bx[l], 'lru_lambda': lru_lambda[l],
            's5_lam_re': s5_lam_re[l], 's5_lam_im': s5_lam_im[l], 's5_log_step': s5_log_step[l],
            's5_b_re': s5_b_re[l], 's5_b_im': s5_b_im[l], 's5_c_re': s5_c_re[l], 's5_c_im': s5_c_im[l],
            's5_d': s5_d[l], 's5_w_glu': s5_w_glu[l], 'mix_g': mix_g[l], 'w_out': w_out[l],
        }
        mod_l = (jax.nn.silu(c) @ w_mod[l] + b_mod[l])[:, None, :]
        mod_c = jax.nn.silu(c_ctx) @ w_mod[l] + b_mod[l]
        sh1, sc1, g1, sh2, sc2, g2 = jnp.split(mod_l, 6, axis=-1)
        csh1, csc1, cg1, csh2, csc2, cg2 = jnp.split(mod_c, 6, axis=-1)
        h_l = modulate(rms_norm(x, norm1_g[l]), sh1, sc1)
        h_c = modulate(rms_norm(ctx, norm1_g[l]), csh1, csc1)
        o_l, o_c = token_mixers(h_l, h_c, p, rope, need_ctx)
        x = x + g1 * o_l
        x = x + g2 * conv_glu_ffn(modulate(rms_norm(x, norm2_g[l]), sh2, sc2),
                                  ffn_w_up[l], ffn_conv_w[l], ffn_conv_b[l], ffn_w_down[l])
        if need_ctx:
            ctx = ctx + cg1 * o_c
            ctx = ctx + cg2 * conv_glu_ffn(modulate(rms_norm(ctx, norm2_g[l]), csh2, csc2),
                                          ffn_w_up[l], ffn_conv_w[l], ffn_conv_b[l], ffn_w_down[l])
    return x
```

```python
import functools

import jax
import jax.numpy as jnp
from jax import lax
from jax.experimental import pallas as pl
from jax.experimental.pallas import tpu as pltpu

D_MODEL = 1024
HEAD_DIM = 64
N_Q_HEADS = 8
N_KV_HEADS = 2
GQA_GROUP = N_Q_HEADS // N_KV_HEADS
ATTN_W = N_Q_HEADS * HEAD_DIM
KV_W = N_KV_HEADS * HEAD_DIM
WINDOW = 128
BLOCK = 128
GRID_W = 64
ROPE_BASE = 10000.0
LRU_W = 256
LRU_HEADS = 4
LRU_HEAD_W = LRU_W // LRU_HEADS
LRU_C = 8.0
S5_W = 256
S5_GROUP = 16
S5_GROUPS = S5_W // S5_GROUP
S5_STATE = 64
S5_FLAT = S5_GROUPS * S5_STATE
N_DIR = 2
IN_W = ATTN_W + 2 * KV_W + 2 * LRU_W + S5_W
D_FF = 2816
NEG = -1e30
EPS = 1e-6

LANES = 128
SUBLANES = 8
TOKEN_TILE = 256
SCAN_CHUNK = 128
SCAN_PITCH = SCAN_CHUNK + SUBLANES
FF_CHUNK = 256
VMEM_LIMIT = 56 * 1024 * 1024

_F32 = jnp.float32
_BF16 = jnp.bfloat16


def _cparams(n_axes):
    return pltpu.CompilerParams(dimension_semantics=("arbitrary",) * n_axes,
                                vmem_limit_bytes=VMEM_LIMIT)


def _const_spec(shape):
    nd = len(shape)
    return pl.BlockSpec(shape, lambda *_: (0,) * nd)


def _dot(a, b):
    return jnp.dot(a, b, preferred_element_type=_F32)


def _lane_iota(shape):
    return lax.broadcasted_iota(jnp.int32, shape, len(shape) - 1)


def _mod_kernel(c_ref, w_ref, b_ref, o_ref):
    a = jax.nn.silu(c_ref[...])
    o_ref[...] = jnp.dot(a, w_ref[...], preferred_element_type=_F32,
                         precision=lax.Precision.HIGHEST) + b_ref[...]


def _modulation(cvecs, w_mod, b_mod):
    depth, d, n = w_mod.shape
    nt = 1536
    return pl.pallas_call(
        _mod_kernel,
        grid=(depth, n // nt),
        in_specs=[pl.BlockSpec((SUBLANES, d), lambda l, j: (0, 0)),
                  pl.BlockSpec((None, d, nt), lambda l, j: (l, 0, j)),
                  pl.BlockSpec((None, 1, nt), lambda l, j: (l, 0, j))],
        out_specs=pl.BlockSpec((None, SUBLANES, nt), lambda l, j: (l, 0, j)),
        out_shape=jax.ShapeDtypeStruct((depth, SUBLANES, n), _F32),
        compiler_params=_cparams(2),
        name="modulation",
    )(cvecs, w_mod, b_mod.reshape(depth, 1, n))


def _rms_rows(x):
    return x * lax.rsqrt(jnp.mean(x * x, axis=-1, keepdims=True) + EPS)


def _head_norm_rope(t, gain, cos, sin):
    lane = _lane_iota(t.shape)
    low = lane < HEAD_DIM
    sq = t * t
    s_low = jnp.sum(jnp.where(low, sq, 0.0), axis=-1, keepdims=True)
    s_all = jnp.sum(sq, axis=-1, keepdims=True)
    ms = jnp.where(low, s_low, s_all - s_low) * (1.0 / HEAD_DIM)
    tn = t * lax.rsqrt(ms + EPS) * gain
    partner = jnp.where((lane % 32) < 16, pltpu.roll(tn, LANES - 16, 1), pltpu.roll(tn, 16, 1))
    return tn * cos + partner * sin


def _inproj_kernel(x_ref, mod_ref, g_ref, w_ref, qg_ref, kg_ref, cos_ref, sin_ref,
                   q_ref, k_ref, v_ref, gate_ref, xl_ref, u_ref):
    x = x_ref[...]
    scale = g_ref[...] * (1.0 + mod_ref[1:2, :])
    h = _rms_rows(x) * scale + mod_ref[0:1, :]
    proj = _dot(h.astype(_BF16), w_ref[...])
    cos = cos_ref[...]
    sin = sin_ref[...]
    for g in range(GQA_GROUP):
        blk = proj[:, g * LANES:(g + 1) * LANES]
        q_ref[:, g * LANES:(g + 1) * LANES] = _head_norm_rope(blk, qg_ref[...], cos, sin).astype(_BF16)
    o = ATTN_W
    k_ref[...] = _head_norm_rope(proj[:, o:o + KV_W], kg_ref[...], cos, sin).astype(_BF16)
    o += KV_W
    v_ref[...] = proj[:, o:o + KV_W].astype(_BF16)
    o += KV_W
    gate_ref[...] = proj[:, o:o + LRU_W]
    o += LRU_W
    xl_ref[...] = proj[:, o:o + LRU_W]
    o += LRU_W
    u_ref[...] = proj[:, o:o + S5_W]


def _inproj(x_all, msel, norm_g, w_in_p, qg, kg, cos_t, sin_t, n_lat_tiles):
    b, s, d = x_all.shape
    tm = TOKEN_TILE
    tok = lambda w: pl.BlockSpec((None, tm, w), lambda bi, i: (bi, i, 0))
    outs = [(ATTN_W, _BF16), (KV_W, _BF16), (KV_W, _BF16), (LRU_W, _F32), (LRU_W, _F32), (S5_W, _F32)]
    return pl.pallas_call(
        _inproj_kernel,
        grid=(b, s // tm),
        in_specs=[tok(d),
                  pl.BlockSpec((None, None, 6, d), lambda bi, i: (bi, (i >= n_lat_tiles).astype(jnp.int32), 0, 0)),
                  _const_spec((1, d)),
                  _const_spec((d, IN_W)),
                  _const_spec((1, LANES)),
                  _const_spec((1, LANES)),
                  pl.BlockSpec((tm, LANES), lambda bi, i: (i, 0)),
                  pl.BlockSpec((tm, LANES), lambda bi, i: (i, 0))],
        out_specs=[tok(w) for w, _ in outs],
        out_shape=[jax.ShapeDtypeStruct((b, s, w), dt) for w, dt in outs],
        compiler_params=_cparams(2),
        name="inproj",
    )(x_all, msel, norm_g, w_in_p, qg, kg, cos_t, sin_t)


def _attn_kernel(sink_ref, q_ref, kp_ref, kc_ref, kn_ref, kx_ref, vp_ref, vc_ref, vn_ref, vx_ref,
                 o_ref, *, n_lat_blocks):
    qi = pl.program_id(1)
    is_lat = qi < n_lat_blocks
    col_lo = jnp.where(is_lat, jnp.where(qi == 0, BLOCK, 0), 0)
    col_hi = jnp.where(is_lat, jnp.where(qi == n_lat_blocks - 1, 2 * BLOCK, 3 * BLOCK), 0)
    n_ctx = kx_ref.shape[0]
    n_keys = 3 * BLOCK + n_ctx
    row = lax.broadcasted_iota(jnp.int32, (BLOCK, n_keys), 0)
    col = lax.broadcasted_iota(jnp.int32, (BLOCK, n_keys), 1)
    valid = ((col >= row) & (col <= row + 2 * WINDOW) & (col >= col_lo) & (col < col_hi)) | (col >= 3 * BLOCK)
    bias = jnp.where(valid, 0.0, NEG)

    keys = jnp.concatenate([kp_ref[...], kc_ref[...], kn_ref[...], kx_ref[...]], axis=0)
    vals = jnp.concatenate([vp_ref[...], vc_ref[...], vn_ref[...], vx_ref[...]], axis=0)
    lane = _lane_iota((BLOCK, LANES))
    low = lane < HEAD_DIM
    zero = jnp.zeros((BLOCK, LANES), _BF16)
    slabs = []
    for g in range(GQA_GROUP):
        blk = q_ref[:, g * LANES:(g + 1) * LANES]
        slabs.append(jnp.where(low, blk, zero))
        slabs.append(jnp.where(low, zero, blk))
    qst = jnp.concatenate(slabs, axis=0)
    s = lax.dot_general(qst, keys, (((1,), (1,)), ((), ())), preferred_element_type=_F32)
    for g in range(GQA_GROUP):
        outs = []
        for hk in range(N_KV_HEADS):
            r0 = (g * N_KV_HEADS + hk) * BLOCK
            sink = sink_ref[hk * GQA_GROUP + g]
            sc = s[r0:r0 + BLOCK, :] + bias
            m = jnp.maximum(jnp.max(sc, axis=-1, keepdims=True), sink)
            p = jnp.exp(sc - m)
            denom = jnp.sum(p, axis=-1, keepdims=True) + jnp.exp(sink - m)
            outs.append(_dot(p.astype(_BF16), vals) / denom)
        o_ref[:, g * LANES:(g + 1) * LANES] = jnp.where(low, outs[0], outs[1]).astype(_BF16)


def _attention(q, k, v, sink, n_lat, n_ctx, with_ctx_queries):
    b, s, _ = q.shape
    nlb = n_lat // BLOCK
    nq = s // BLOCK if with_ctx_queries else nlb
    last = s // BLOCK - 1
    blk = lambda f: pl.BlockSpec((None, BLOCK, KV_W), lambda bi, i: (bi, f(i), 0))
    prev_i = lambda i: jnp.maximum(i - 1, 0)
    next_i = lambda i: jnp.minimum(i + 1, last)
    ctx_spec = pl.BlockSpec((None, n_ctx, KV_W), lambda bi, i: (bi, n_lat // n_ctx, 0))
    return pl.pallas_call(
        functools.partial(_attn_kernel, n_lat_blocks=nlb),
        grid=(b, nq),
        in_specs=[pl.BlockSpec(memory_space=pltpu.SMEM),
                  pl.BlockSpec((None, BLOCK, ATTN_W), lambda bi, i: (bi, i, 0)),
                  blk(prev_i), blk(lambda i: i), blk(next_i), ctx_spec,
                  blk(prev_i), blk(lambda i: i), blk(next_i), ctx_spec],
        out_specs=pl.BlockSpec((None, BLOCK, ATTN_W), lambda bi, i: (bi, i, 0)),
        out_shape=jax.ShapeDtypeStruct((b, s, ATTN_W), _BF16),
        compiler_params=_cparams(2),
        name="attention",
    )(sink, q, k, k, k, k, v, v, v, v)


def _scan_kernel(xf_ref, xfp_ref, xfn_ref, uf_ref, xb_ref, xbp_ref, xbn_ref, ub_ref,
                 cw_ref, cb_ref, wa_ref, wx_ref, ba_ref, bx_ref, nsp_ref,
                 bd_ref, cm_ref, ar_ref, ai_ref,
                 lf_ref, lb_ref, yf_ref, yb_ref,
                 s5_buf, lru_buf, s5_state, lru_state, *, n_lat_chunks, n_ctx_chunks):
    i = pl.program_id(0)
    n_b = xf_ref.shape[0]
    t_len = SCAN_CHUNK
    pitch = SCAN_PITCH
    n_slab = 2 * S5_FLAT // (2 * LANES)
    half_w = n_slab * LANES
    n_tot = n_lat_chunks + n_ctx_chunks
    chunk_f = jnp.where(i < n_ctx_chunks, n_lat_chunks + i, i - n_ctx_chunks)
    chunk_b = n_tot - 1 - i

    @pl.when(i == 0)
    def _():
        s5_state[...] = jnp.zeros_like(s5_state)
        lru_state[...] = jnp.zeros_like(lru_state)

    def seg_first(c):
        return (c == 0) | (c == n_lat_chunks)

    def seg_last(c):
        return (c == n_lat_chunks - 1) | (c == n_tot - 1)

    dirs = ((0, xf_ref, xfp_ref, xfn_ref, uf_ref, chunk_f), (1, xb_ref, xbp_ref, xbn_ref, ub_ref, chunk_b))
    for d, x_ref, xp_ref, xn_ref, u_ref, chunk in dirs:
        keep_prev = jnp.where(seg_first(chunk), 0.0, 1.0)
        keep_next = jnp.where(seg_last(chunk), 0.0, 1.0)
        for b in range(n_b):
            xe = jnp.concatenate([xp_ref[b] * keep_prev, x_ref[b], xn_ref[b] * keep_next], axis=0)
            n_e = t_len + 2 * SUBLANES
            acc = cb_ref[...] + cw_ref[1:2, :] * x_ref[b]
            for j, shift in ((0, 1), (2, n_e - 1), (3, n_e - 2)):
                acc = acc + cw_ref[j:j + 1, :] * pltpu.roll(xe, shift, 0)[SUBLANES:SUBLANES + t_len]
            xc = acc
            xcb = xc.astype(_BF16)
            r = jax.nn.sigmoid(_dot(xcb, wa_ref[d]) + ba_ref[d])
            gi = jax.nn.sigmoid(_dot(xcb, wx_ref[d]) + bx_ref[d])
            log_a = nsp_ref[d] * r
            a = jnp.exp(log_a)
            bco = jnp.sqrt(-jnp.tanh(log_a) * (a * a + 1.0)) * gi * xc
            drive = _dot(u_ref[b].astype(_BF16), bd_ref[d])
            for h in range(2):
                r0 = (b * 2 + h) * pitch
                lru_buf[d, 0, r0:r0 + t_len, :] = a[:, h * LANES:(h + 1) * LANES]
                lru_buf[d, 1, r0:r0 + t_len, :] = bco[:, h * LANES:(h + 1) * LANES]
                for sl in range(n_slab):
                    c0 = h * half_w + sl * LANES
                    s5_buf[d, sl, r0:r0 + t_len, :] = drive[:, c0:c0 + LANES]

    n_k = n_slab // 2
    coef = [[(ar_ref[d, :, k * LANES:(k + 1) * LANES], ai_ref[d, :, k * LANES:(k + 1) * LANES])
             for k in range(n_k)] for d in range(N_DIR)]

    def step(it, carry):
        new = []
        for d in range(N_DIR):
            t = it if d == 0 else t_len - 1 - it
            rows = pl.ds(t, SUBLANES, stride=pitch)
            hs, hl = carry[d]
            nhs = [None] * n_slab
            for k in range(n_k):
                ar, ai = coef[d][k]
                hr, hi = hs[k], hs[n_k + k]
                nr = ar * hr - ai * hi + s5_buf[d, k, rows, :]
                ni = ar * hi + ai * hr + s5_buf[d, n_k + k, rows, :]
                s5_buf[d, k, rows, :] = nr
                s5_buf[d, n_k + k, rows, :] = ni
                nhs[k], nhs[n_k + k] = nr, ni
            nhl = lru_buf[d, 0, rows, :] * hl + lru_buf[d, 1, rows, :]
            lru_buf[d, 1, rows, :] = nhl
            new.append((tuple(nhs), nhl))
        return tuple(new)

    init = tuple((tuple(s5_state[d, sl] for sl in range(n_slab)), lru_state[d]) for d in range(N_DIR))
    fin = lax.fori_loop(0, t_len, step, init, unroll=2)
    for d in range(N_DIR):
        for sl in range(n_slab):
            s5_state[d, sl] = fin[d][0][sl]
        lru_state[d] = fin[d][1]

    for d, l_ref, y_ref in ((0, lf_ref, yf_ref), (1, lb_ref, yb_ref)):
        for b in range(n_b):
            y = None
            halves = []
            for h in range(2):
                r0 = (b * 2 + h) * pitch
                halves.append(lru_buf[d, 1, r0:r0 + t_len, :])
                hm = jnp.concatenate([s5_buf[d, sl, r0:r0 + t_len, :] for sl in range(n_slab)], axis=1)
                part = _dot(hm.astype(_BF16), cm_ref[d, h * half_w:(h + 1) * half_w, :])
                y = part if y is None else y + part
            y_ref[b] = y
            l_ref[b] = jnp.concatenate(halves, axis=1)


def _scans(xl, u, sp, n_lat, n_ctx):
    b, s, _ = xl.shape
    t = SCAN_CHUNK
    nl, nc = n_lat // t, n_ctx // t
    n_tot = nl + nc
    n8 = s // SUBLANES
    per8 = t // SUBLANES
    cf = lambda i: jnp.where(i < nc, nl + i, i - nc)
    cb = lambda i: n_tot - 1 - i
    main = lambda f: pl.BlockSpec((b, t, LRU_W), lambda i: (0, f(i), 0))
    prev = lambda f: pl.BlockSpec((b, SUBLANES, LRU_W), lambda i: (0, jnp.maximum(f(i) * per8 - 1, 0), 0))
    nxt = lambda f: pl.BlockSpec((b, SUBLANES, LRU_W), lambda i: (0, jnp.minimum((f(i) + 1) * per8, n8 - 1), 0))
    n_slab = 2 * S5_FLAT // (2 * LANES)
    weights = [sp["conv_w"], sp["conv_b"], sp["wa"], sp["wx"], sp["ba"], sp["bx"], sp["nsp"],
               sp["bd"], sp["cm"], sp["ar"], sp["ai"]]
    out_sd = jax.ShapeDtypeStruct((b, s, LRU_W), _F32)
    return pl.pallas_call(
        functools.partial(_scan_kernel, n_lat_chunks=nl, n_ctx_chunks=nc),
        grid=(n_tot,),
        in_specs=[main(cf), prev(cf), nxt(cf), main(cf), main(cb), prev(cb), nxt(cb), main(cb)]
                 + [_const_spec(w.shape) for w in weights],
        out_specs=[main(cf), main(cb), main(cf), main(cb)],
        out_shape=[out_sd] * 4,
        scratch_shapes=[pltpu.VMEM((N_DIR, n_slab, SUBLANES * SCAN_PITCH, LANES), _F32),
                        pltpu.VMEM((N_DIR, 2, SUBLANES * SCAN_PITCH, LANES), _F32),
                        pltpu.VMEM((N_DIR, n_slab, SUBLANES, LANES), _F32),
                        pltpu.VMEM((N_DIR, SUBLANES, LANES), _F32)],
        compiler_params=_cparams(1),
        name="scans",
    )(xl, xl, xl, u, xl, xl, xl, u, *weights)


def _merge_kernel(x_ref, mod_ref, attn_ref, gate_ref, lf_ref, lb_ref, yf_ref, yb_ref, u_ref,
                  mg_ref, dskip_ref, wglu_ref, wout_ref, o_ref):
    o_lru = jax.nn.gelu(gate_ref[...]) * (lf_ref[...] + lb_ref[...])
    y = u_ref[...] * dskip_ref[...] + yf_ref[...] + yb_ref[...]
    z = jax.nn.gelu(y)
    o_s5 = z * jax.nn.sigmoid(_dot(z.astype(_BF16), wglu_ref[...]))
    mg = mg_ref[...]
    parts = [_rms_rows(attn_ref[...].astype(_F32)) * mg[:, :ATTN_W],
             _rms_rows(o_lru) * mg[:, ATTN_W:ATTN_W + LRU_W],
             _rms_rows(o_s5) * mg[:, ATTN_W + LRU_W:]]
    cat = jnp.concatenate([p.astype(_BF16) for p in parts], axis=1)
    o_ref[...] = x_ref[...] + mod_ref[2:3, :] * _dot(cat, wout_ref[...])


def _merge(x_all, msel, attn, gate, lf, lb, yf, yb, u, mg_p, dskip, w_glu, w_out_p, n_lat_tiles, n_tiles):
    b, _, d = x_all.shape
    tm = TOKEN_TILE
    tok = lambda w: pl.BlockSpec((None, tm, w), lambda bi, i: (bi, i, 0))
    return pl.pallas_call(
        _merge_kernel,
        grid=(b, n_tiles),
        in_specs=[tok(d),
                  pl.BlockSpec((None, None, 6, d), lambda bi, i: (bi, (i >= n_lat_tiles).astype(jnp.int32), 0, 0)),
                  tok(ATTN_W), tok(LRU_W), tok(LRU_W), tok(LRU_W), tok(S5_W), tok(S5_W), tok(S5_W),
                  _const_spec((1, d)), _const_spec((1, S5_W)),
                  _const_spec((S5_W, S5_W)), _const_spec((d, d))],
        out_specs=tok(d),
        out_shape=jax.ShapeDtypeStruct((b, n_tiles * tm, d), _F32),
        compiler_params=_cparams(2),
        name="merge",
    )(x_all, msel, attn, gate, lf, lb, yf, yb, u, mg_p, dskip, w_glu, w_out_p)


def _ffn_kernel(x_ref, xp_ref, xn_ref, mod_ref, g_ref, wg_ref, wv_ref, cw_ref, cb_ref, wd_ref,
                o_ref, acc_ref, *, n_lat_tiles, n_tiles):
    i = pl.program_id(1)
    tm = x_ref.shape[0]
    n_e = tm + 2 * SUBLANES
    first = (i == 0) | (i == n_lat_tiles)
    last = (i == n_lat_tiles - 1) | (i == n_tiles - 1)
    x = x_ref[...]
    xe = jnp.concatenate([xp_ref[...], x, xn_ref[...]], axis=0)
    scale = g_ref[...] * (1.0 + mod_ref[4:5, :])
    he = (_rms_rows(xe) * scale + mod_ref[3:4, :]).astype(_BF16)
    hc = he[SUBLANES:SUBLANES + tm]
    rowe = lax.broadcasted_iota(jnp.int32, (n_e, 1), 0)
    keep = jnp.where(((rowe < SUBLANES) & first) | ((rowe >= SUBLANES + tm) & last), 0.0, 1.0)
    acc_ref[...] = jnp.zeros_like(acc_ref)

    def chunk(f, carry):
        ge = _dot(he, wg_ref[f]) * keep
        cw = cw_ref[f]
        conv = (cb_ref[f] + cw[1:2, :] * ge[SUBLANES:SUBLANES + tm]
                + cw[0:1, :] * pltpu.roll(ge, 1, 0)[SUBLANES:SUBLANES + tm]
                + cw[2:3, :] * pltpu.roll(ge, n_e - 1, 0)[SUBLANES:SUBLANES + tm])
        act = jax.nn.gelu(conv) * _dot(hc, wv_ref[f])
        acc_ref[...] += _dot(act.astype(_BF16), wd_ref[f])
        return carry

    lax.fori_loop(0, wg_ref.shape[0], chunk, 0)
    o_ref[...] = x + mod_ref[5:6, :] * acc_ref[...]


def _ffn(x1, msel, norm_g, wg, wv, cw, cb, wd, n_lat_tiles, n_tiles):
    b, s, d = x1.shape
    tm = TOKEN_TILE
    per8 = tm // SUBLANES
    n8 = s // SUBLANES
    tok = pl.BlockSpec((None, tm, d), lambda bi, i: (bi, i, 0))
    return pl.pallas_call(
        functools.partial(_ffn_kernel, n_lat_tiles=n_lat_tiles, n_tiles=n_tiles),
        grid=(b, n_tiles),
        in_specs=[tok,
                  pl.BlockSpec((None, SUBLANES, d), lambda bi, i: (bi, jnp.maximum(i * per8 - 1, 0), 0)),
                  pl.BlockSpec((None, SUBLANES, d), lambda bi, i: (bi, jnp.minimum((i + 1) * per8, n8 - 1), 0)),
                  pl.BlockSpec((None, None, 6, d), lambda bi, i: (bi, (i >= n_lat_tiles).astype(jnp.int32), 0, 0)),
                  _const_spec((1, d)),
                  _const_spec(wg.shape), _const_spec(wv.shape), _const_spec(cw.shape), _const_spec(cb.shape),
                  _const_spec(wd.shape)],
        out_specs=tok,
        out_shape=jax.ShapeDtypeStruct((b, n_tiles * tm, d), _F32),
        scratch_shapes=[pltpu.VMEM((tm, d), _F32)],
        compiler_params=_cparams(2),
        name="ffn",
    )(x1, x1, x1, msel, norm_g, wg, wv, cw, cb, wd)


def _rope_tables(n_lat, n_ctx):
    t = jnp.arange(n_lat)
    row = (t // GRID_W).astype(_F32)
    col = (t % GRID_W).astype(_F32)
    n_freq = HEAD_DIM // 4
    inv = ROPE_BASE ** (-jnp.arange(n_freq, dtype=_F32) / n_freq)
    ang_r = row[:, None] * inv
    ang_c = col[:, None] * inv
    cos_h = jnp.concatenate([jnp.cos(ang_r), jnp.cos(ang_r), jnp.cos(ang_c), jnp.cos(ang_c)], axis=1)
    sin_h = jnp.concatenate([-jnp.sin(ang_r), jnp.sin(ang_r), -jnp.sin(ang_c), jnp.sin(ang_c)], axis=1)
    cos_t = jnp.concatenate([jnp.tile(cos_h, (1, 2)), jnp.ones((n_ctx, LANES), _F32)], axis=0)
    sin_t = jnp.concatenate([jnp.tile(sin_h, (1, 2)), jnp.zeros((n_ctx, LANES), _F32)], axis=0)
    return cos_t, sin_t


def _q_column_order():
    idx = []
    for g in range(GQA_GROUP):
        for hk in range(N_KV_HEADS):
            h = hk * GQA_GROUP + g
            idx.extend(range(h * HEAD_DIM, (h + 1) * HEAD_DIM))
    return jnp.array(idx, jnp.int32)


def _block_diag(blocks):
    n, r, c = blocks.shape
    eye = jnp.eye(n, dtype=blocks.dtype)
    return (eye[:, None, :, None] * blocks[:, :, None, :]).reshape(n * r, n * c)


def _s5_params(lam_re, lam_im, log_step, b_re, b_im, c_re, c_im):
    lr = jnp.minimum(lam_re.astype(_F32), -1e-4)
    li = lam_im.astype(_F32)
    dt = jnp.exp(log_step.astype(_F32))[:, None]
    mag = jnp.exp(lr * dt)
    ab_re = mag * jnp.cos(li * dt)
    ab_im = mag * jnp.sin(li * dt)
    nr = ab_re - 1
    den = lr * lr + li * li
    cr = ((nr * lr + ab_im * li) / den)[..., None]
    ci = ((ab_im * lr - nr * li) / den)[..., None]
    br = b_re.astype(_F32)
    bi = b_im.astype(_F32)
    bb_re = cr * br - ci * bi
    bb_im = cr * bi + ci * br
    half = S5_FLAT // 2
    d_re = _block_diag(jnp.swapaxes(bb_re, 1, 2))
    d_im = _block_diag(jnp.swapaxes(bb_im, 1, 2))
    bd = jnp.concatenate([d_re[:, :half], d_im[:, :half], d_re[:, half:], d_im[:, half:]], axis=1)
    r_re = _block_diag(jnp.swapaxes(c_re.astype(_F32), 1, 2))
    r_im = -_block_diag(jnp.swapaxes(c_im.astype(_F32), 1, 2))
    cm = jnp.concatenate([r_re[:half], r_im[:half], r_re[half:], r_im[half:]], axis=0)
    a_re = jnp.tile(ab_re.reshape(2, half), (SUBLANES // 2, 1))
    a_im = jnp.tile(ab_im.reshape(2, half), (SUBLANES // 2, 1))
    return bd.astype(_BF16), cm.astype(_BF16), a_re, a_im


def kernel(x, c, ctx, c_ctx, w_mod, b_mod, norm1_g, norm2_g, w_in, q_norm_g, k_norm_g, attn_sink,
           lru_conv_w, lru_conv_b, lru_wa, lru_ba, lru_wx, lru_bx, lru_lambda,
           s5_lam_re, s5_lam_im, s5_log_step, s5_b_re, s5_b_im, s5_c_re, s5_c_im, s5_d, s5_w_glu,
           mix_g, w_out, ffn_w_up, ffn_conv_w, ffn_conv_b, ffn_w_down):
    n_b, n_lat, d = x.shape
    n_ctx = ctx.shape[1]
    depth = w_mod.shape[0]
    assert d == D_MODEL and n_b * 2 == SUBLANES
    assert n_lat % TOKEN_TILE == 0 and n_ctx % TOKEN_TILE == 0 and n_lat % n_ctx == 0
    n_lat_tiles = n_lat // TOKEN_TILE
    n_all_tiles = (n_lat + n_ctx) // TOKEN_TILE

    cvecs = jnp.concatenate([c, c_ctx[None], jnp.zeros((SUBLANES - n_b - 1, d), _F32)], axis=0)
    mod = _modulation(cvecs, w_mod, b_mod).reshape(depth, SUBLANES, 6, d)
    cos_t, sin_t = _rope_tables(n_lat, n_ctx)
    q_cols = _q_column_order()
    in_cols = jnp.concatenate([q_cols, jnp.arange(ATTN_W, IN_W, dtype=jnp.int32)])
    mix_rows = jnp.concatenate([q_cols, jnp.arange(ATTN_W, D_MODEL, dtype=jnp.int32)])
    n_ff = D_FF // FF_CHUNK

    x_all = jnp.concatenate([x, ctx], axis=1)
    for l in range(depth):
        last = l == depth - 1
        msel = jnp.stack([mod[l, :n_b], jnp.broadcast_to(mod[l, n_b], (n_b, 6, d))], axis=1)
        w_in_p = w_in[l][:, in_cols].astype(_BF16)
        qg = jnp.tile(q_norm_g[l], 2)[None] * (HEAD_DIM ** -0.5)
        kg = jnp.tile(k_norm_g[l], 2)[None]
        q, k, v, gate, xl, u = _inproj(x_all, msel, norm1_g[l][None], w_in_p, qg, kg, cos_t, sin_t, n_lat_tiles)
        attn = _attention(q, k, v, attn_sink[l], n_lat, n_ctx, with_ctx_queries=not last)

        s5 = [_s5_params(s5_lam_re[l, dd], s5_lam_im[l, dd], s5_log_step[l, dd], s5_b_re[l, dd], s5_b_im[l, dd],
                         s5_c_re[l, dd], s5_c_im[l, dd]) for dd in range(N_DIR)]
        sp = {
            "conv_w": lru_conv_w[l], "conv_b": lru_conv_b[l][None],
            "wa": jnp.stack([_block_diag(lru_wa[l, dd]) for dd in range(N_DIR)]).astype(_BF16),
            "wx": jnp.stack([_block_diag(lru_wx[l, dd]) for dd in range(N_DIR)]).astype(_BF16),
            "ba": lru_ba[l][:, None, :], "bx": lru_bx[l][:, None, :],
            "nsp": (-LRU_C * jax.nn.softplus(-lru_lambda[l].astype(_F32)))[:, None, :],
            "bd": jnp.stack([p[0] for p in s5]), "cm": jnp.stack([p[1] for p in s5]),
            "ar": jnp.stack([p[2] for p in s5]), "ai": jnp.stack([p[3] for p in s5]),
        }
        lf, lb, yf, yb = _scans(xl, u, sp, n_lat, n_ctx)

        n_tiles = n_lat_tiles if last else n_all_tiles
        x1 = _merge(x_all, msel, attn, gate, lf, lb, yf, yb, u, mix_g[l][mix_rows][None], s5_d[l][None],
                    s5_w_glu[l].astype(_BF16), w_out[l][mix_rows].astype(_BF16), n_lat_tiles, n_tiles)
        w_up = ffn_w_up[l].astype(_BF16)
        wg = w_up[:, :D_FF].reshape(d, n_ff, FF_CHUNK).transpose(1, 0, 2)
        wv = w_up[:, D_FF:].reshape(d, n_ff, FF_CHUNK).transpose(1, 0, 2)
        cw = ffn_conv_w[l].reshape(3, n_ff, FF_CHUNK).transpose(1, 0, 2)
        cb = ffn_conv_b[l].reshape(n_ff, 1, FF_CHUNK)
        wd = ffn_w_down[l].astype(_BF16).reshape(n_ff, FF_CHUNK, d)
        x_all = _ffn(x1, msel, norm2_g[l][None], wg, wv, cw, cb, wd, n_lat_tiles, n_tiles)
    return x_all
```

```python
import functools

import jax
import jax.numpy as jnp
from jax import lax
from jax.experimental import pallas as pl
from jax.experimental.pallas import tpu as pltpu

D_MODEL = 1024
HEAD_DIM = 64
N_Q_HEADS = 8
N_KV_HEADS = 2
GQA_GROUP = N_Q_HEADS // N_KV_HEADS
ATTN_W = N_Q_HEADS * HEAD_DIM
KV_W = N_KV_HEADS * HEAD_DIM
WINDOW = 128
BLOCK = 128
GRID_W = 64
ROPE_BASE = 10000.0
LRU_W = 256
LRU_HEADS = 4
LRU_HEAD_W = LRU_W // LRU_HEADS
LRU_C = 8.0
S5_W = 256
S5_GROUP = 16
S5_GROUPS = S5_W // S5_GROUP
S5_STATE = 64
S5_FLAT = S5_GROUPS * S5_STATE
N_DIR = 2
IN_W = ATTN_W + 2 * KV_W + 2 * LRU_W + S5_W
D_FF = 2816
NEG = -1e30
EPS = 1e-6
LOG2E = 1.4426950408889634

LANES = 128
SUBLANES = 8
TOKEN_TILE = 256
SCAN_CHUNK = 128
SCAN_PITCH = SCAN_CHUNK + SUBLANES // 2
FF_CHUNK = 256
VMEM_LIMIT = 56 * 1024 * 1024

_F32 = jnp.float32
_BF16 = jnp.bfloat16


def _cparams(n_axes):
    return pltpu.CompilerParams(dimension_semantics=("arbitrary",) * n_axes,
                                vmem_limit_bytes=VMEM_LIMIT)


def _const_spec(shape):
    nd = len(shape)
    return pl.BlockSpec(shape, lambda *_: (0,) * nd)


def _dot(a, b):
    return jnp.dot(a, b, preferred_element_type=_F32)


def _lane_iota(shape):
    return lax.broadcasted_iota(jnp.int32, shape, len(shape) - 1)


def _mod_kernel(c_ref, w_ref, b_ref, o_ref):
    a = jax.nn.silu(c_ref[...])
    o_ref[...] = jnp.dot(a, w_ref[...], preferred_element_type=_F32,
                         precision=lax.Precision.HIGHEST) + b_ref[...]


def _modulation(cvecs, w_mod, b_mod):
    depth, d, n = w_mod.shape
    nt = 1536
    return pl.pallas_call(
        _mod_kernel,
        grid=(depth, n // nt),
        in_specs=[pl.BlockSpec((SUBLANES, d), lambda l, j: (0, 0)),
                  pl.BlockSpec((None, d, nt), lambda l, j: (l, 0, j)),
                  pl.BlockSpec((None, 1, nt), lambda l, j: (l, 0, j))],
        out_specs=pl.BlockSpec((None, SUBLANES, nt), lambda l, j: (l, 0, j)),
        out_shape=jax.ShapeDtypeStruct((depth, SUBLANES, n), _F32),
        compiler_params=_cparams(2),
        name="modulation",
    )(cvecs, w_mod, b_mod.reshape(depth, 1, n))


def _rms_rows(x):
    return x * lax.rsqrt(jnp.mean(x * x, axis=-1, keepdims=True) + EPS)


def _head_norm_rope(t, gain, cos, sin):
    lane = _lane_iota(t.shape)
    low = lane < HEAD_DIM
    sq = t * t
    s_low = jnp.sum(jnp.where(low, sq, 0.0), axis=-1, keepdims=True)
    s_all = jnp.sum(sq, axis=-1, keepdims=True)
    ms = jnp.where(low, s_low, s_all - s_low) * (1.0 / HEAD_DIM)
    tn = t * lax.rsqrt(ms + EPS) * gain
    partner = jnp.where((lane % 32) < 16, pltpu.roll(tn, LANES - 16, 1), pltpu.roll(tn, 16, 1))
    return tn * cos + partner * sin


def _tile_specs(tm, d, n_lat_tiles, ctx_tile0):
    lat = pl.BlockSpec((None, tm, d), lambda bi, i: (bi, jnp.minimum(i, n_lat_tiles - 1), 0))
    ctx = pl.BlockSpec((None, tm, d), lambda bi, i: (bi, ctx_tile0 + jnp.maximum(i - n_lat_tiles, 0), 0))
    return lat, ctx


def _inproj_kernel(x_ref, c_ref, mod_ref, g_ref, w_ref, qg_ref, kg_ref, cos_ref, sin_ref,
                   q_ref, k_ref, v_ref, gate_ref, xl_ref, u_ref, *, n_lat_tiles):
    x = jnp.where(pl.program_id(1) >= n_lat_tiles, c_ref[...], x_ref[...])
    scale = g_ref[...] * (1.0 + mod_ref[1:2, :])
    h = _rms_rows(x) * scale + mod_ref[0:1, :]
    proj = _dot(h.astype(_BF16), w_ref[...])
    cos = cos_ref[...]
    sin = sin_ref[...]
    for g in range(GQA_GROUP):
        blk = proj[:, g * LANES:(g + 1) * LANES]
        q_ref[:, g * LANES:(g + 1) * LANES] = _head_norm_rope(blk, qg_ref[...], cos, sin).astype(_BF16)
    o = ATTN_W
    k_ref[...] = _head_norm_rope(proj[:, o:o + KV_W], kg_ref[...], cos, sin).astype(_BF16)
    o += KV_W
    v_ref[...] = proj[:, o:o + KV_W].astype(_BF16)
    o += KV_W
    gate_ref[...] = proj[:, o:o + LRU_W]
    o += LRU_W
    xl_ref[...] = proj[:, o:o + LRU_W]
    o += LRU_W
    u_ref[...] = proj[:, o:o + S5_W]


def _inproj(x_src, c_src, ctx_tile0, msel, norm_g, w_in_p, qg, kg, cos_t, sin_t, n_lat_tiles, n_tiles):
    b, _, d = x_src.shape
    tm = TOKEN_TILE
    s = n_tiles * tm
    tok = lambda w: pl.BlockSpec((None, tm, w), lambda bi, i: (bi, i, 0))
    outs = [(ATTN_W, _BF16), (KV_W, _BF16), (KV_W, _BF16), (LRU_W, _F32), (LRU_W, _F32), (S5_W, _F32)]
    return pl.pallas_call(
        functools.partial(_inproj_kernel, n_lat_tiles=n_lat_tiles),
        grid=(b, n_tiles),
        in_specs=[*_tile_specs(tm, d, n_lat_tiles, ctx_tile0),
                  pl.BlockSpec((None, None, 6, d), lambda bi, i: (bi, (i >= n_lat_tiles).astype(jnp.int32), 0, 0)),
                  _const_spec((1, d)),
                  _const_spec((d, IN_W)),
                  _const_spec((1, LANES)),
                  _const_spec((1, LANES)),
                  pl.BlockSpec((tm, LANES), lambda bi, i: (i, 0)),
                  pl.BlockSpec((tm, LANES), lambda bi, i: (i, 0))],
        out_specs=[tok(w) for w, _ in outs],
        out_shape=[jax.ShapeDtypeStruct((b, s, w), dt) for w, dt in outs],
        compiler_params=_cparams(2),
        name="inproj",
    )(x_src, c_src, msel, norm_g, w_in_p, qg, kg, cos_t, sin_t)


def _attn_kernel(sink_ref, q_ref, kp_ref, kc_ref, kn_ref, kx_ref, vp_ref, vc_ref, vn_ref, vx_ref,
                 o_ref, *, n_lat_blocks):
    qi = pl.program_id(1)
    is_lat = qi < n_lat_blocks
    row = lax.broadcasted_iota(jnp.int32, (BLOCK, BLOCK), 0)
    col = lax.broadcasted_iota(jnp.int32, (BLOCK, BLOCK), 1)
    bias_prev = jnp.where(col >= row, jnp.where(is_lat & (qi > 0), 0.0, NEG), NEG)
    bias_next = jnp.where(col <= row, jnp.where(qi < n_lat_blocks - 1, 0.0, NEG), NEG)
    bias_cur = jnp.where(is_lat, 0.0, NEG)

    keys = jnp.concatenate([kp_ref[...], kc_ref[...], kn_ref[...], kx_ref[...]], axis=0)
    vals = jnp.concatenate([vp_ref[...], vc_ref[...], vn_ref[...], vx_ref[...]], axis=0)
    lane = _lane_iota((BLOCK, LANES))
    low = lane < HEAD_DIM
    low_k = _lane_iota(vals.shape) < HEAD_DIM
    one = jnp.ones(vals.shape, _BF16)
    vals_ext = (jnp.where(low_k, vals, one), jnp.where(low_k, one, vals))
    zero = jnp.zeros((BLOCK, LANES), _BF16)
    slabs = []
    for g in range(GQA_GROUP):
        blk = q_ref[:, g * LANES:(g + 1) * LANES]
        slabs.append(jnp.where(low, blk, zero))
        slabs.append(jnp.where(low, zero, blk))
    qst = jnp.concatenate(slabs, axis=0)
    s = lax.dot_general(qst, keys, (((1,), (1,)), ((), ())), preferred_element_type=_F32)
    for g in range(GQA_GROUP):
        res, esink = [], []
        for hk in range(N_KV_HEADS):
            r0 = (g * N_KV_HEADS + hk) * BLOCK
            sink = sink_ref[hk * GQA_GROUP + g] * LOG2E
            sl = s[r0:r0 + BLOCK, :]
            sc = jnp.concatenate([sl[:, :BLOCK] + bias_prev, sl[:, BLOCK:2 * BLOCK] + bias_cur,
                                  sl[:, 2 * BLOCK:3 * BLOCK] + bias_next, sl[:, 3 * BLOCK:]], axis=1)
            m = jnp.maximum(jnp.max(sc, axis=-1, keepdims=True), sink)
            p = jnp.exp2(sc - m)
            res.append(_dot(p.astype(_BF16), vals_ext[hk]))
            esink.append(jnp.exp2(sink - m))
        num = jnp.where(low, res[0], res[1])
        den = pltpu.roll(jnp.where(low, res[1], res[0]), HEAD_DIM, 1) + jnp.where(low, esink[0], esink[1])
        o_ref[:, g * LANES:(g + 1) * LANES] = (num / den).astype(_BF16)


def _attention(q, k, v, sink, n_lat, n_ctx, with_ctx_queries):
    b, s, _ = q.shape
    nlb = n_lat // BLOCK
    nq = s // BLOCK if with_ctx_queries else nlb
    last = s // BLOCK - 1
    blk = lambda f: pl.BlockSpec((None, BLOCK, KV_W), lambda bi, i: (bi, f(i), 0))
    prev_i = lambda i: jnp.maximum(i - 1, 0)
    next_i = lambda i: jnp.minimum(i + 1, last)
    ctx_spec = pl.BlockSpec((None, n_ctx, KV_W), lambda bi, i: (bi, n_lat // n_ctx, 0))
    return pl.pallas_call(
        functools.partial(_attn_kernel, n_lat_blocks=nlb),
        grid=(b, nq),
        in_specs=[pl.BlockSpec(memory_space=pltpu.SMEM),
                  pl.BlockSpec((None, BLOCK, ATTN_W), lambda bi, i: (bi, i, 0)),
                  blk(prev_i), blk(lambda i: i), blk(next_i), ctx_spec,
                  blk(prev_i), blk(lambda i: i), blk(next_i), ctx_spec],
        out_specs=pl.BlockSpec((None, BLOCK, ATTN_W), lambda bi, i: (bi, i, 0)),
        out_shape=jax.ShapeDtypeStruct((b, nq * BLOCK, ATTN_W), _BF16),
        compiler_params=_cparams(2),
        name="attention",
    )(sink, q, k, k, k, k, v, v, v, v)


def _shift_rows(x, n_rows_out, shift):
    pad = jnp.zeros((n_rows_out - x.shape[0], x.shape[1]), x.dtype)
    return pltpu.roll(jnp.concatenate([x, pad], axis=0), shift, 0)


def _scan_kernel(xf_ref, xfp_ref, xfn_ref, uf_ref, xb_ref, xbp_ref, xbn_ref, ub_ref,
                 cw_ref, cb_ref, wa_ref, wx_ref, ba_ref, bx_ref, nsp_ref,
                 bd_ref, cm_ref, ar_ref, ai_ref,
                 lf_ref, lb_ref, yf_ref, yb_ref,
                 s5_in, s5_out, lru_in, lru_out, s5_state, lru_state, *, n_lat_chunks, n_ctx_chunks):
    i = pl.program_id(0)
    n_b = xf_ref.shape[0]
    t_len = SCAN_CHUNK
    pitch = SCAN_PITCH
    off = pitch - t_len
    t_win = t_len + SUBLANES
    n_slab = 2 * S5_FLAT // (2 * LANES)
    half_w = n_slab * LANES
    n_tot = n_lat_chunks + n_ctx_chunks
    chunk_f = jnp.where(i < n_ctx_chunks, n_lat_chunks + i, i - n_ctx_chunks)
    chunk_b = n_tot - 1 - i

    @pl.when(i == 0)
    def _():
        s5_state[...] = jnp.zeros_like(s5_state)
        lru_state[...] = jnp.zeros_like(lru_state)
        s5_out[...] = jnp.zeros_like(s5_out)
        lru_out[...] = jnp.zeros_like(lru_out)

    def seg_first(c):
        return (c == 0) | (c == n_lat_chunks)

    def seg_last(c):
        return (c == n_lat_chunks - 1) | (c == n_tot - 1)

    def even_rows(b):
        return pl.ds(2 * b * pitch, t_len)

    def odd_rows(b):
        return pl.ds((2 * b + 1) * pitch - off, t_win)

    dirs = ((0, xf_ref, xfp_ref, xfn_ref, uf_ref, chunk_f), (1, xb_ref, xbp_ref, xbn_ref, ub_ref, chunk_b))
    for d, x_ref, xp_ref, xn_ref, u_ref, chunk in dirs:
        keep_prev = jnp.where(seg_first(chunk), 0.0, 1.0)
        keep_next = jnp.where(seg_last(chunk), 0.0, 1.0)
        n_e = t_len + 2 * SUBLANES
        xcs = []
        for b in range(n_b):
            xe = jnp.concatenate([x_ref[b], xn_ref[b] * keep_next, xp_ref[b] * keep_prev], axis=0)
            acc = cb_ref[...] + cw_ref[1:2, :] * x_ref[b]
            for j, shift in ((0, 1), (2, n_e - 1), (3, n_e - 2)):
                acc = acc + cw_ref[j:j + 1, :] * pltpu.roll(xe, shift, 0)[:t_len]
            xcs.append(acc)
        xc = jnp.concatenate(xcs, axis=0)
        xcb = xc.astype(_BF16)
        r = jax.nn.sigmoid(_dot(xcb, wa_ref[d]) + ba_ref[d])
        gi = jax.nn.sigmoid(_dot(xcb, wx_ref[d]) + bx_ref[d])
        log_a = nsp_ref[d] * r
        a = jnp.exp(log_a)
        bco = jnp.sqrt(-jnp.tanh(log_a) * (a * a + 1.0)) * gi * xc
        for b in range(n_b):
            rb = slice(b * t_len, (b + 1) * t_len)
            lru_in[d, 0, even_rows(b), :] = a[rb, :LANES]
            lru_in[d, 1, even_rows(b), :] = bco[rb, :LANES]
            lru_in[d, 0, odd_rows(b), :] = _shift_rows(a[rb, LANES:], t_win, off)
            lru_in[d, 1, odd_rows(b), :] = _shift_rows(bco[rb, LANES:], t_win, off)
        u_even = jnp.concatenate([u_ref[b] for b in range(n_b)], axis=0)
        u_odd = jnp.concatenate([_shift_rows(u_ref[b], t_win, off) for b in range(n_b)], axis=0)
        drive_even = _dot(u_even.astype(_BF16), bd_ref[d, :, :half_w])
        drive_odd = _dot(u_odd.astype(_BF16), bd_ref[d, :, half_w:])
        for b in range(n_b):
            for sl in range(n_slab):
                cs = slice(sl * LANES, (sl + 1) * LANES)
                s5_in[d, sl, even_rows(b), :] = drive_even[b * t_len:(b + 1) * t_len, cs]
                s5_in[d, sl, odd_rows(b), :] = drive_odd[b * t_win:(b + 1) * t_win, cs]

    n_k = n_slab // 2
    coef = [[(ar_ref[d, :, k * LANES:(k + 1) * LANES], ai_ref[d, :, k * LANES:(k + 1) * LANES])
             for k in range(n_k)] for d in range(N_DIR)]

    def step(it, carry):
        new = []
        for d in range(N_DIR):
            t = it if d == 0 else t_len - 1 - it
            rows = pl.ds(t, SUBLANES, stride=pitch)
            hs, hl = carry[d]
            nhs = [None] * n_slab
            for k in range(n_k):
                ar, ai = coef[d][k]
                hr, hi = hs[k], hs[n_k + k]
                nr = ar * hr - ai * hi + s5_in[d, k, rows, :]
                ni = ar * hi + ai * hr + s5_in[d, n_k + k, rows, :]
                s5_out[d, k, rows, :] = nr
                s5_out[d, n_k + k, rows, :] = ni
                nhs[k], nhs[n_k + k] = nr, ni
            nhl = lru_in[d, 0, rows, :] * hl + lru_in[d, 1, rows, :]
            lru_out[d, rows, :] = nhl
            new.append((tuple(nhs), nhl))
        return tuple(new)

    init = tuple((tuple(s5_state[d, sl] for sl in range(n_slab)), lru_state[d]) for d in range(N_DIR))
    fin = lax.fori_loop(0, t_len, step, init, unroll=4)
    for d in range(N_DIR):
        for sl in range(n_slab):
            s5_state[d, sl] = fin[d][0][sl]
        lru_state[d] = fin[d][1]

    for d, l_ref, y_ref in ((0, lf_ref, yf_ref), (1, lb_ref, yb_ref)):
        h_even = jnp.concatenate(
            [jnp.concatenate([s5_out[d, sl, even_rows(b), :] for sl in range(n_slab)], axis=1)
             for b in range(n_b)], axis=0)
        h_odd = jnp.concatenate(
            [jnp.concatenate([s5_out[d, sl, odd_rows(b), :] for sl in range(n_slab)], axis=1)
             for b in range(n_b)], axis=0)
        y_even = _dot(h_even.astype(_BF16), cm_ref[d, :half_w, :])
        y_odd = _dot(h_odd.astype(_BF16), cm_ref[d, half_w:, :])
        for b in range(n_b):
            y_ref[b] = (y_even[b * t_len:(b + 1) * t_len]
                        + pltpu.roll(y_odd[b * t_win:(b + 1) * t_win], t_win - off, 0)[:t_len])
            l_ref[b] = jnp.concatenate(
                [lru_out[d, even_rows(b), :],
                 pltpu.roll(lru_out[d, odd_rows(b), :], t_win - off, 0)[:t_len]], axis=1)


def _scans(xl, u, sp, n_lat, n_ctx):
    b, s, _ = xl.shape
    t = SCAN_CHUNK
    nl, nc = n_lat // t, n_ctx // t
    n_tot = nl + nc
    n8 = s // SUBLANES
    per8 = t // SUBLANES
    cf = lambda i: jnp.where(i < nc, nl + i, i - nc)
    cb = lambda i: n_tot - 1 - i
    main = lambda f: pl.BlockSpec((b, t, LRU_W), lambda i: (0, f(i), 0))
    prev = lambda f: pl.BlockSpec((b, SUBLANES, LRU_W), lambda i: (0, jnp.maximum(f(i) * per8 - 1, 0), 0))
    nxt = lambda f: pl.BlockSpec((b, SUBLANES, LRU_W), lambda i: (0, jnp.minimum((f(i) + 1) * per8, n8 - 1), 0))
    n_slab = 2 * S5_FLAT // (2 * LANES)
    n_rows = SUBLANES * SCAN_PITCH
    weights = [sp["conv_w"], sp["conv_b"], sp["wa"], sp["wx"], sp["ba"], sp["bx"], sp["nsp"],
               sp["bd"], sp["cm"], sp["ar"], sp["ai"]]
    out_sd = jax.ShapeDtypeStruct((b, s, LRU_W), _F32)
    return pl.pallas_call(
        functools.partial(_scan_kernel, n_lat_chunks=nl, n_ctx_chunks=nc),
        grid=(n_tot,),
        in_specs=[main(cf), prev(cf), nxt(cf), main(cf), main(cb), prev(cb), nxt(cb), main(cb)]
                 + [_const_spec(w.shape) for w in weights],
        out_specs=[main(cf), main(cb), main(cf), main(cb)],
        out_shape=[out_sd] * 4,
        scratch_shapes=[pltpu.VMEM((N_DIR, n_slab, n_rows, LANES), _F32),
                        pltpu.VMEM((N_DIR, n_slab, n_rows, LANES), _F32),
                        pltpu.VMEM((N_DIR, 2, n_rows, LANES), _F32),
                        pltpu.VMEM((N_DIR, n_rows, LANES), _F32),
                        pltpu.VMEM((N_DIR, n_slab, SUBLANES, LANES), _F32),
                        pltpu.VMEM((N_DIR, SUBLANES, LANES), _F32)],
        compiler_params=_cparams(1),
        name="scans",
    )(xl, xl, xl, u, xl, xl, xl, u, *weights)


def _merge_kernel(x_ref, c_ref, mod_ref, attn_ref, gate_ref, lf_ref, lb_ref, yf_ref, yb_ref, u_ref,
                  mg_ref, dskip_ref, wglu_ref, wout_ref, o_ref, *, n_lat_tiles):
    x = jnp.where(pl.program_id(1) >= n_lat_tiles, c_ref[...], x_ref[...])
    o_lru = jax.nn.gelu(gate_ref[...]) * (lf_ref[...] + lb_ref[...])
    y = u_ref[...] * dskip_ref[...] + yf_ref[...] + yb_ref[...]
    z = jax.nn.gelu(y)
    o_s5 = z * jax.nn.sigmoid(_dot(z.astype(_BF16), wglu_ref[...]))
    mg = mg_ref[...]
    parts = [_rms_rows(attn_ref[...].astype(_F32)) * mg[:, :ATTN_W],
             _rms_rows(o_lru) * mg[:, ATTN_W:ATTN_W + LRU_W],
             _rms_rows(o_s5) * mg[:, ATTN_W + LRU_W:]]
    cat = jnp.concatenate([p.astype(_BF16) for p in parts], axis=1)
    o_ref[...] = x + mod_ref[2:3, :] * _dot(cat, wout_ref[...])


def _merge(x_src, c_src, ctx_tile0, msel, attn, gate, lf, lb, yf, yb, u, mg_p, dskip, w_glu, w_out_p,
           n_lat_tiles, n_tiles):
    b, _, d = x_src.shape
    tm = TOKEN_TILE
    tok = lambda w: pl.BlockSpec((None, tm, w), lambda bi, i: (bi, i, 0))
    return pl.pallas_call(
        functools.partial(_merge_kernel, n_lat_tiles=n_lat_tiles),
        grid=(b, n_tiles),
        in_specs=[*_tile_specs(tm, d, n_lat_tiles, ctx_tile0),
                  pl.BlockSpec((None, None, 6, d), lambda bi, i: (bi, (i >= n_lat_tiles).astype(jnp.int32), 0, 0)),
                  tok(ATTN_W), tok(LRU_W), tok(LRU_W), tok(LRU_W), tok(S5_W), tok(S5_W), tok(S5_W),
                  _const_spec((1, d)), _const_spec((1, S5_W)),
                  _const_spec((S5_W, S5_W)), _const_spec((d, d))],
        out_specs=tok(d),
        out_shape=jax.ShapeDtypeStruct((b, n_tiles * tm, d), _F32),
        compiler_params=_cparams(2),
        name="merge",
    )(x_src, c_src, msel, attn, gate, lf, lb, yf, yb, u, mg_p, dskip, w_glu, w_out_p)


def _ffn_kernel(x_ref, xp_ref, xn_ref, mod_ref, g_ref, wup_ref, cw_ref, cb_ref, wd_ref,
                o_ref, act_ref, *, n_lat_tiles, n_tiles):
    i = pl.program_id(1)
    tm = x_ref.shape[0]
    n_e = tm + 2 * SUBLANES
    first = (i == 0) | (i == n_lat_tiles)
    last = (i == n_lat_tiles - 1) | (i == n_tiles - 1)
    x = x_ref[...]
    xe = jnp.concatenate([x, xn_ref[...], xp_ref[...]], axis=0)
    scale = g_ref[...] * (1.0 + mod_ref[4:5, :])
    rowe = lax.broadcasted_iota(jnp.int32, (n_e, 1), 0)
    pad_row = ((rowe >= jnp.where(first, tm + SUBLANES, n_e))
               | ((rowe >= tm) & (rowe < jnp.where(last, tm + SUBLANES, tm))))
    he = jnp.where(pad_row, 0.0, _rms_rows(xe) * scale + mod_ref[3:4, :]).astype(_BF16)
    hc = he[:tm]
    for f in range(D_FF // FF_CHUNK):
        cs = slice(f * FF_CHUNK, (f + 1) * FF_CHUNK)
        ge = _dot(he, wup_ref[:, cs])
        gv = _dot(hc, wup_ref[:, D_FF + f * FF_CHUNK:D_FF + (f + 1) * FF_CHUNK])
        conv = (cb_ref[:, cs] + cw_ref[1:2, cs] * ge[:tm]
                + cw_ref[0:1, cs] * pltpu.roll(ge, 1, 0)[:tm]
                + cw_ref[2:3, cs] * pltpu.roll(ge, n_e - 1, 0)[:tm])
        act_ref[:, cs] = (jax.nn.gelu(conv) * gv).astype(_BF16)
    o_ref[...] = x + mod_ref[5:6, :] * _dot(act_ref[...], wd_ref[...])


def _ffn(x1, msel, norm_g, w_up, cw, cb, wd, n_lat_tiles, n_tiles):
    b, s, d = x1.shape
    tm = TOKEN_TILE
    per8 = tm // SUBLANES
    n8 = s // SUBLANES
    tok = pl.BlockSpec((None, tm, d), lambda bi, i: (bi, i, 0))
    resident = lambda a: pl.BlockSpec(a.shape, lambda *_: (0,) * a.ndim, pipeline_mode=pl.Buffered(1))
    return pl.pallas_call(
        functools.partial(_ffn_kernel, n_lat_tiles=n_lat_tiles, n_tiles=n_tiles),
        grid=(b, n_tiles),
        in_specs=[tok,
                  pl.BlockSpec((None, SUBLANES, d), lambda bi, i: (bi, jnp.maximum(i * per8 - 1, 0), 0)),
                  pl.BlockSpec((None, SUBLANES, d), lambda bi, i: (bi, jnp.minimum((i + 1) * per8, n8 - 1), 0)),
                  pl.BlockSpec((None, None, 6, d), lambda bi, i: (bi, (i >= n_lat_tiles).astype(jnp.int32), 0, 0)),
                  _const_spec((1, d)),
                  resident(w_up), _const_spec(cw.shape), _const_spec(cb.shape), resident(wd)],
        out_specs=tok,
        out_shape=jax.ShapeDtypeStruct((b, n_tiles * tm, d), _F32),
        scratch_shapes=[pltpu.VMEM((tm, D_FF), _BF16)],
        compiler_params=_cparams(2),
        name="ffn",
    )(x1, x1, x1, msel, norm_g, w_up, cw, cb, wd)


def _rope_tables(n_lat, n_ctx):
    t = jnp.arange(n_lat)
    row = (t // GRID_W).astype(_F32)
    col = (t % GRID_W).astype(_F32)
    n_freq = HEAD_DIM // 4
    inv = ROPE_BASE ** (-jnp.arange(n_freq, dtype=_F32) / n_freq)
    ang_r = row[:, None] * inv
    ang_c = col[:, None] * inv
    cos_h = jnp.concatenate([jnp.cos(ang_r), jnp.cos(ang_r), jnp.cos(ang_c), jnp.cos(ang_c)], axis=1)
    sin_h = jnp.concatenate([-jnp.sin(ang_r), jnp.sin(ang_r), -jnp.sin(ang_c), jnp.sin(ang_c)], axis=1)
    cos_t = jnp.concatenate([jnp.tile(cos_h, (1, 2)), jnp.ones((n_ctx, LANES), _F32)], axis=0)
    sin_t = jnp.concatenate([jnp.tile(sin_h, (1, 2)), jnp.zeros((n_ctx, LANES), _F32)], axis=0)
    return cos_t, sin_t


def _q_column_order():
    idx = []
    for g in range(GQA_GROUP):
        for hk in range(N_KV_HEADS):
            h = hk * GQA_GROUP + g
            idx.extend(range(h * HEAD_DIM, (h + 1) * HEAD_DIM))
    return jnp.array(idx, jnp.int32)


def _block_diag(blocks):
    n, r, c = blocks.shape
    eye = jnp.eye(n, dtype=blocks.dtype)
    return (eye[:, None, :, None] * blocks[:, :, None, :]).reshape(n * r, n * c)


def _s5_params(lam_re, lam_im, log_step, b_re, b_im, c_re, c_im):
    lr = jnp.minimum(lam_re.astype(_F32), -1e-4)
    li = lam_im.astype(_F32)
    dt = jnp.exp(log_step.astype(_F32))[:, None]
    mag = jnp.exp(lr * dt)
    ab_re = mag * jnp.cos(li * dt)
    ab_im = mag * jnp.sin(li * dt)
    nr = ab_re - 1
    den = lr * lr + li * li
    cr = ((nr * lr + ab_im * li) / den)[..., None]
    ci = ((ab_im * lr - nr * li) / den)[..., None]
    br = b_re.astype(_F32)
    bi = b_im.astype(_F32)
    bb_re = cr * br - ci * bi
    bb_im = cr * bi + ci * br
    half = S5_FLAT // 2
    d_re = _block_diag(jnp.swapaxes(bb_re, 1, 2))
    d_im = _block_diag(jnp.swapaxes(bb_im, 1, 2))
    bd = jnp.concatenate([d_re[:, :half], d_im[:, :half], d_re[:, half:], d_im[:, half:]], axis=1)
    r_re = _block_diag(jnp.swapaxes(c_re.astype(_F32), 1, 2))
    r_im = -_block_diag(jnp.swapaxes(c_im.astype(_F32), 1, 2))
    cm = jnp.concatenate([r_re[:half], r_im[:half], r_re[half:], r_im[half:]], axis=0)
    a_re = jnp.tile(ab_re.reshape(2, half), (SUBLANES // 2, 1))
    a_im = jnp.tile(ab_im.reshape(2, half), (SUBLANES // 2, 1))
    return bd.astype(_BF16), cm.astype(_BF16), a_re, a_im


def kernel(x, c, ctx, c_ctx, w_mod, b_mod, norm1_g, norm2_g, w_in, q_norm_g, k_norm_g, attn_sink,
           lru_conv_w, lru_conv_b, lru_wa, lru_ba, lru_wx, lru_bx, lru_lambda,
           s5_lam_re, s5_lam_im, s5_log_step, s5_b_re, s5_b_im, s5_c_re, s5_c_im, s5_d, s5_w_glu,
           mix_g, w_out, ffn_w_up, ffn_conv_w, ffn_conv_b, ffn_w_down):
    n_b, n_lat, d = x.shape
    n_ctx = ctx.shape[1]
    depth = w_mod.shape[0]
    assert d == D_MODEL and n_b * 2 == SUBLANES
    assert n_lat % TOKEN_TILE == 0 and n_ctx % TOKEN_TILE == 0 and n_lat % n_ctx == 0
    n_lat_tiles = n_lat // TOKEN_TILE
    n_all_tiles = (n_lat + n_ctx) // TOKEN_TILE

    cvecs = jnp.concatenate([c, c_ctx[None], jnp.zeros((SUBLANES - n_b - 1, d), _F32)], axis=0)
    mod = _modulation(cvecs, w_mod, b_mod).reshape(depth, SUBLANES, 6, d)
    cos_t, sin_t = _rope_tables(n_lat, n_ctx)
    q_cols = _q_column_order()
    in_cols = jnp.concatenate([q_cols, jnp.arange(ATTN_W, IN_W, dtype=jnp.int32)])
    mix_rows = jnp.concatenate([q_cols, jnp.arange(ATTN_W, D_MODEL, dtype=jnp.int32)])

    x_src, c_src, ctx_tile0 = x, ctx, 0
    for l in range(depth):
        last = l == depth - 1
        msel = jnp.stack([mod[l, :n_b], jnp.broadcast_to(mod[l, n_b], (n_b, 6, d))], axis=1)
        w_in_p = w_in[l][:, in_cols].astype(_BF16)
        qg = jnp.tile(q_norm_g[l], 2)[None] * (HEAD_DIM ** -0.5 * LOG2E)
        kg = jnp.tile(k_norm_g[l], 2)[None]
        q, k, v, gate, xl, u = _inproj(x_src, c_src, ctx_tile0, msel, norm1_g[l][None], w_in_p, qg, kg,
                                       cos_t, sin_t, n_lat_tiles, n_all_tiles)
        attn = _attention(q, k, v, attn_sink[l], n_lat, n_ctx, with_ctx_queries=not last)

        s5 = [_s5_params(s5_lam_re[l, dd], s5_lam_im[l, dd], s5_log_step[l, dd], s5_b_re[l, dd], s5_b_im[l, dd],
                         s5_c_re[l, dd], s5_c_im[l, dd]) for dd in range(N_DIR)]
        sp = {
            "conv_w": lru_conv_w[l], "conv_b": lru_conv_b[l][None],
            "wa": jnp.stack([_block_diag(lru_wa[l, dd]) for dd in range(N_DIR)]).astype(_BF16),
            "wx": jnp.stack([_block_diag(lru_wx[l, dd]) for dd in range(N_DIR)]).astype(_BF16),
            "ba": lru_ba[l][:, None, :], "bx": lru_bx[l][:, None, :],
            "nsp": (-LRU_C * jax.nn.softplus(-lru_lambda[l].astype(_F32)))[:, None, :],
            "bd": jnp.stack([p[0] for p in s5]), "cm": jnp.stack([p[1] for p in s5]),
            "ar": jnp.stack([p[2] for p in s5]), "ai": jnp.stack([p[3] for p in s5]),
        }
        lf, lb, yf, yb = _scans(xl, u, sp, n_lat, n_ctx)

        n_tiles = n_lat_tiles if last else n_all_tiles
        x1 = _merge(x_src, c_src, ctx_tile0, msel, attn, gate, lf, lb, yf, yb, u, mix_g[l][mix_rows][None],
                    s5_d[l][None], s5_w_glu[l].astype(_BF16), w_out[l][mix_rows].astype(_BF16),
                    n_lat_tiles, n_tiles)
        x_all = _ffn(x1, msel, norm2_g[l][None], ffn_w_up[l].astype(_BF16), ffn_conv_w[l], ffn_conv_b[l][None],
                     ffn_w_down[l].astype(_BF16), n_lat_tiles, n_tiles)
        x_src, c_src, ctx_tile0 = x_all, x_all, n_lat_tiles
    return x_all
```

```python
import functools

import jax
import jax.numpy as jnp
from jax import lax
from jax.experimental import pallas as pl
from jax.experimental.pallas import tpu as pltpu

D_MODEL = 1024
HEAD_DIM = 64
N_Q_HEADS = 8
N_KV_HEADS = 2
GQA_GROUP = N_Q_HEADS // N_KV_HEADS
ATTN_W = N_Q_HEADS * HEAD_DIM
KV_W = N_KV_HEADS * HEAD_DIM
WINDOW = 128
BLOCK = 128
GRID_W = 64
ROPE_BASE = 10000.0
LRU_W = 256
LRU_HEADS = 4
LRU_HEAD_W = LRU_W // LRU_HEADS
LRU_C = 8.0
S5_W = 256
S5_GROUP = 16
S5_GROUPS = S5_W // S5_GROUP
S5_STATE = 64
S5_FLAT = S5_GROUPS * S5_STATE
N_DIR = 2
IN_W = ATTN_W + 2 * KV_W + 2 * LRU_W + S5_W
D_FF = 2816
NEG = -1e30
EPS = 1e-6
LOG2E = 1.4426950408889634

LANES = 128
SUBLANES = 8
TOKEN_TILE = 256
SCAN_CHUNK = 128
SCAN_PITCH = SCAN_CHUNK + SUBLANES // 2
FF_CHUNK = 256
VMEM_LIMIT = 56 * 1024 * 1024

_F32 = jnp.float32
_BF16 = jnp.bfloat16


def _cparams(n_axes):
    return pltpu.CompilerParams(dimension_semantics=("arbitrary",) * n_axes,
                                vmem_limit_bytes=VMEM_LIMIT)


def _const_spec(shape):
    nd = len(shape)
    return pl.BlockSpec(shape, lambda *_: (0,) * nd)


def _dot(a, b):
    return jnp.dot(a, b, preferred_element_type=_F32)


def _lane_iota(shape):
    return lax.broadcasted_iota(jnp.int32, shape, len(shape) - 1)


def _mod_kernel(c_ref, w_ref, b_ref, o_ref):
    a = jax.nn.silu(c_ref[...])
    o_ref[...] = jnp.dot(a, w_ref[...], preferred_element_type=_F32,
                         precision=lax.Precision.HIGHEST) + b_ref[...]


def _modulation(cvecs, w_mod, b_mod):
    depth, d, n = w_mod.shape
    nt = 1536
    return pl.pallas_call(
        _mod_kernel,
        grid=(depth, n // nt),
        in_specs=[pl.BlockSpec((SUBLANES, d), lambda l, j: (0, 0)),
                  pl.BlockSpec((None, d, nt), lambda l, j: (l, 0, j)),
                  pl.BlockSpec((None, 1, nt), lambda l, j: (l, 0, j))],
        out_specs=pl.BlockSpec((None, SUBLANES, nt), lambda l, j: (l, 0, j)),
        out_shape=jax.ShapeDtypeStruct((depth, SUBLANES, n), _F32),
        compiler_params=_cparams(2),
        name="modulation",
    )(cvecs, w_mod, b_mod.reshape(depth, 1, n))


def _rms_rows(x):
    return x * lax.rsqrt(jnp.mean(x * x, axis=-1, keepdims=True) + EPS)


def _head_norm_rope(t, gain, cos, sin):
    lane = _lane_iota(t.shape)
    low = lane < HEAD_DIM
    sq = t * t
    s_low = jnp.sum(jnp.where(low, sq, 0.0), axis=-1, keepdims=True)
    s_all = jnp.sum(sq, axis=-1, keepdims=True)
    ms = jnp.where(low, s_low, s_all - s_low) * (1.0 / HEAD_DIM)
    tn = t * lax.rsqrt(ms + EPS) * gain
    partner = jnp.where((lane % 32) < 16, pltpu.roll(tn, LANES - 16, 1), pltpu.roll(tn, 16, 1))
    return tn * cos + partner * sin


def _tile_specs(tm, d, n_lat_tiles, ctx_tile0):
    lat = pl.BlockSpec((None, tm, d), lambda bi, i: (bi, jnp.minimum(i, n_lat_tiles - 1), 0))
    ctx = pl.BlockSpec((None, tm, d), lambda bi, i: (bi, ctx_tile0 + jnp.maximum(i - n_lat_tiles, 0), 0))
    return lat, ctx


def _inproj_kernel(x_ref, c_ref, mod_ref, g_ref, w_ref, qg_ref, kg_ref, cos_ref, sin_ref,
                   q_ref, kv_ref, gate_ref, xl_ref, u_ref, *, n_lat_tiles):
    x = jnp.where(pl.program_id(1) >= n_lat_tiles, c_ref[...], x_ref[...])
    scale = g_ref[...] * (1.0 + mod_ref[1:2, :])
    h = _rms_rows(x) * scale + mod_ref[0:1, :]
    proj = _dot(h.astype(_BF16), w_ref[...])
    cos = cos_ref[...]
    sin = sin_ref[...]
    for g in range(GQA_GROUP):
        blk = proj[:, g * LANES:(g + 1) * LANES]
        q_ref[:, g * LANES:(g + 1) * LANES] = _head_norm_rope(blk, qg_ref[...], cos, sin).astype(_BF16)
    o = ATTN_W
    kv_ref[:, :KV_W] = _head_norm_rope(proj[:, o:o + KV_W], kg_ref[...], cos, sin).astype(_BF16)
    o += KV_W
    kv_ref[:, KV_W:] = proj[:, o:o + KV_W].astype(_BF16)
    o += KV_W
    gate_ref[...] = proj[:, o:o + LRU_W]
    o += LRU_W
    xl_ref[...] = proj[:, o:o + LRU_W]
    o += LRU_W
    u_ref[...] = proj[:, o:o + S5_W]


def _inproj(x_src, c_src, ctx_tile0, msel, norm_g, w_in_p, qg, kg, cos_t, sin_t, n_lat_tiles, n_tiles):
    b, _, d = x_src.shape
    tm = TOKEN_TILE
    s = n_tiles * tm
    tok = lambda w: pl.BlockSpec((None, tm, w), lambda bi, i: (bi, i, 0))
    outs = [(ATTN_W, _BF16), (2 * KV_W, _BF16), (LRU_W, _F32), (LRU_W, _F32), (S5_W, _F32)]
    return pl.pallas_call(
        functools.partial(_inproj_kernel, n_lat_tiles=n_lat_tiles),
        grid=(b, n_tiles),
        in_specs=[*_tile_specs(tm, d, n_lat_tiles, ctx_tile0),
                  pl.BlockSpec((None, None, 6, d), lambda bi, i: (bi, (i >= n_lat_tiles).astype(jnp.int32), 0, 0)),
                  _const_spec((1, d)),
                  _const_spec((d, IN_W)),
                  _const_spec((1, LANES)),
                  _const_spec((1, LANES)),
                  pl.BlockSpec((tm, LANES), lambda bi, i: (i, 0)),
                  pl.BlockSpec((tm, LANES), lambda bi, i: (i, 0))],
        out_specs=[tok(w) for w, _ in outs],
        out_shape=[jax.ShapeDtypeStruct((b, s, w), dt) for w, dt in outs],
        compiler_params=_cparams(2),
        name="inproj",
    )(x_src, c_src, msel, norm_g, w_in_p, qg, kg, cos_t, sin_t)


def _attend(q_blk, pieces, sink_ref):
    n_slab = N_KV_HEADS * GQA_GROUP
    lane = _lane_iota((BLOCK, LANES))
    low = lane < HEAD_DIM
    zero = jnp.zeros((BLOCK, LANES), _BF16)
    slabs = [[None] * GQA_GROUP for _ in range(N_KV_HEADS)]
    for g in range(GQA_GROUP):
        blk = q_blk[:, g * LANES:(g + 1) * LANES]
        slabs[0][g] = jnp.where(low, blk, zero)
        slabs[1][g] = jnp.where(low, zero, blk)
    qst = jnp.concatenate(slabs[0] + slabs[1], axis=0)
    half = GQA_GROUP * BLOCK
    m = jnp.concatenate([jnp.full((BLOCK, LANES), sink_ref[s] * LOG2E, _F32) for s in range(n_slab)], axis=0)
    lane_h = _lane_iota((half, LANES)) < HEAD_DIM
    acc = [jnp.where(lane_h, 0.0, 1.0), jnp.where(lane_h, 1.0, 0.0)]
    for t0 in range(0, len(pieces), 2):
        tile = pieces[t0:t0 + 2]
        keys = jnp.concatenate([p[0] for p in tile], axis=0)
        vals = jnp.concatenate([p[1] for p in tile], axis=0)
        s = lax.dot_general(qst, keys, (((1,), (1,)), ((), ())), preferred_element_type=_F32)
        if any(p[2] is not None for p in tile):
            bias = jnp.concatenate([p[2] if p[2] is not None else jnp.zeros((BLOCK, BLOCK), _F32) for p in tile],
                                   axis=1)
            s = jnp.concatenate([s[i * BLOCK:(i + 1) * BLOCK] + bias for i in range(n_slab)], axis=0)
        m_new = jnp.maximum(m, jnp.broadcast_to(jnp.max(s, axis=-1, keepdims=True), m.shape))
        alpha = jnp.exp2(m - m_new)
        p = jnp.exp2(s - jnp.concatenate([m_new] * (s.shape[1] // LANES), axis=1)).astype(_BF16)
        m = m_new
        low_v = _lane_iota(vals.shape) < HEAD_DIM
        one = jnp.ones(vals.shape, _BF16)
        vext = (jnp.where(low_v, vals, one), jnp.where(low_v, one, vals))
        for hk in range(N_KV_HEADS):
            rs = slice(hk * half, (hk + 1) * half)
            acc[hk] = alpha[rs] * acc[hk] + _dot(p[rs], vext[hk])
    outs = []
    for g in range(GQA_GROUP):
        a0 = acc[0][g * BLOCK:(g + 1) * BLOCK]
        a1 = acc[1][g * BLOCK:(g + 1) * BLOCK]
        num = jnp.where(low, a0, a1)
        den = pltpu.roll(jnp.where(low, a1, a0), HEAD_DIM, 1)
        outs.append((num / den).astype(_BF16))
    return outs


def _attn_kernel(sink_ref, q_ref, kvp_ref, kvc_ref, kvn_ref, kvx_ref, o_ref, *, n_lat_pairs):
    qi = pl.program_id(1)
    is_lat = qi < n_lat_pairs
    row = lax.broadcasted_iota(jnp.int32, (BLOCK, BLOCK), 0)
    col = lax.broadcasted_iota(jnp.int32, (BLOCK, BLOCK), 1)
    open_ = jnp.where(is_lat, 0.0, NEG)
    tri_prev = jnp.where(col >= row, open_, NEG)
    tri_next = jnp.where(col <= row, open_, NEG)
    full = jnp.full((BLOCK, BLOCK), open_, _F32)
    shut = jnp.full((BLOCK, BLOCK), NEG, _F32)
    has_prev = is_lat & (qi > 0)
    has_next = qi < n_lat_pairs - 1

    def kv(ref, r0=0):
        return ref[r0:r0 + BLOCK, :KV_W], ref[r0:r0 + BLOCK, KV_W:]

    ctx = [kv(kvx_ref, c * BLOCK) + (None,) for c in range(kvx_ref.shape[0] // BLOCK)]
    lo, hi = kv(kvc_ref, 0), kv(kvc_ref, BLOCK)
    first = [kv(kvp_ref) + (jnp.where(has_prev, tri_prev, shut),), lo + (full,), hi + (tri_next,)] + ctx
    second = [lo + (tri_prev,), hi + (full,), kv(kvn_ref) + (jnp.where(has_next, tri_next, shut),)] + ctx
    for half, pieces in enumerate((first, second)):
        outs = _attend(q_ref[half * BLOCK:(half + 1) * BLOCK, :], pieces, sink_ref)
        for g in range(GQA_GROUP):
            o_ref[half * BLOCK:(half + 1) * BLOCK, g * LANES:(g + 1) * LANES] = outs[g]


def _attention(q, kv, sink, n_lat, n_ctx, with_ctx_queries):
    b, s, _ = q.shape
    pair = 2 * BLOCK
    nlp = n_lat // pair
    nq = s // pair if with_ctx_queries else nlp
    last = s // BLOCK - 1
    halo = lambda f: pl.BlockSpec((None, BLOCK, 2 * KV_W), lambda bi, i: (bi, f(i), 0))
    return pl.pallas_call(
        functools.partial(_attn_kernel, n_lat_pairs=nlp),
        grid=(b, nq),
        in_specs=[pl.BlockSpec(memory_space=pltpu.SMEM),
                  pl.BlockSpec((None, pair, ATTN_W), lambda bi, i: (bi, i, 0)),
                  halo(lambda i: jnp.maximum(2 * i - 1, 0)),
                  pl.BlockSpec((None, pair, 2 * KV_W), lambda bi, i: (bi, i, 0)),
                  halo(lambda i: jnp.minimum(2 * i + 2, last)),
                  pl.BlockSpec((None, n_ctx, 2 * KV_W), lambda bi, i: (bi, n_lat // n_ctx, 0))],
        out_specs=pl.BlockSpec((None, pair, ATTN_W), lambda bi, i: (bi, i, 0)),
        out_shape=jax.ShapeDtypeStruct((b, nq * pair, ATTN_W), _BF16),
        compiler_params=_cparams(2),
        name="attention",
    )(sink, q, kv, kv, kv, kv)


def _shift_rows(x, n_rows_out, shift):
    pad = jnp.zeros((n_rows_out - x.shape[0], x.shape[1]), x.dtype)
    return pltpu.roll(jnp.concatenate([x, pad], axis=0), shift, 0)


def _scan_kernel(xf_ref, xfp_ref, xfn_ref, uf_ref, xb_ref, xbp_ref, xbn_ref, ub_ref,
                 cw_ref, cb_ref, wa_ref, wx_ref, ba_ref, bx_ref, nsp_ref,
                 bd_ref, cm_ref, ar_ref, ai_ref,
                 lf_ref, lb_ref, yf_ref, yb_ref,
                 s5_in, s5_out, lru_in, lru_out, s5_state, lru_state, *, n_lat_chunks, n_ctx_chunks):
    i = pl.program_id(0)
    n_b = xf_ref.shape[0]
    t_len = SCAN_CHUNK
    pitch = SCAN_PITCH
    off = pitch - t_len
    t_win = t_len + SUBLANES
    n_slab = 2 * S5_FLAT // (2 * LANES)
    half_w = n_slab * LANES
    n_tot = n_lat_chunks + n_ctx_chunks
    chunk_f = jnp.where(i < n_ctx_chunks, n_lat_chunks + i, i - n_ctx_chunks)
    chunk_b = n_tot - 1 - i

    @pl.when(i == 0)
    def _():
        s5_state[...] = jnp.zeros_like(s5_state)
        lru_state[...] = jnp.zeros_like(lru_state)
        s5_out[...] = jnp.zeros_like(s5_out)
        lru_out[...] = jnp.zeros_like(lru_out)

    def seg_first(c):
        return (c == 0) | (c == n_lat_chunks)

    def seg_last(c):
        return (c == n_lat_chunks - 1) | (c == n_tot - 1)

    def even_rows(b):
        return pl.ds(2 * b * pitch, t_len)

    def odd_rows(b):
        return pl.ds((2 * b + 1) * pitch - off, t_win)

    dirs = ((0, xf_ref, xfp_ref, xfn_ref, uf_ref, chunk_f), (1, xb_ref, xbp_ref, xbn_ref, ub_ref, chunk_b))
    for d, x_ref, xp_ref, xn_ref, u_ref, chunk in dirs:
        keep_prev = jnp.where(seg_first(chunk), 0.0, 1.0)
        keep_next = jnp.where(seg_last(chunk), 0.0, 1.0)
        n_e = t_len + 2 * SUBLANES
        xcs = []
        for b in range(n_b):
            xe = jnp.concatenate([x_ref[b], xn_ref[b] * keep_next, xp_ref[b] * keep_prev], axis=0)
            acc = cb_ref[...] + cw_ref[1:2, :] * x_ref[b]
            for j, shift in ((0, 1), (2, n_e - 1), (3, n_e - 2)):
                acc = acc + cw_ref[j:j + 1, :] * pltpu.roll(xe, shift, 0)[:t_len]
            xcs.append(acc)
        xc = jnp.concatenate(xcs, axis=0)
        xcb = xc.astype(_BF16)
        r = jax.nn.sigmoid(_dot(xcb, wa_ref[d]) + ba_ref[d])
        gi = jax.nn.sigmoid(_dot(xcb, wx_ref[d]) + bx_ref[d])
        log_a = nsp_ref[d] * r
        a = jnp.exp(log_a)
        bco = jnp.sqrt(-jnp.tanh(log_a) * (a * a + 1.0)) * gi * xc
        for b in range(n_b):
            rb = slice(b * t_len, (b + 1) * t_len)
            lru_in[d, 0, even_rows(b), :] = a[rb, :LANES]
            lru_in[d, 1, even_rows(b), :] = bco[rb, :LANES]
            lru_in[d, 0, odd_rows(b), :] = _shift_rows(a[rb, LANES:], t_win, off)
            lru_in[d, 1, odd_rows(b), :] = _shift_rows(bco[rb, LANES:], t_win, off)
        u_even = jnp.concatenate([u_ref[b] for b in range(n_b)], axis=0)
        u_odd = jnp.concatenate([_shift_rows(u_ref[b], t_win, off) for b in range(n_b)], axis=0)
        drive_even = _dot(u_even.astype(_BF16), bd_ref[d, :, :half_w])
        drive_odd = _dot(u_odd.astype(_BF16), bd_ref[d, :, half_w:])
        for b in range(n_b):
            for sl in range(n_slab):
                cs = slice(sl * LANES, (sl + 1) * LANES)
                s5_in[d, sl, even_rows(b), :] = drive_even[b * t_len:(b + 1) * t_len, cs]
                s5_in[d, sl, odd_rows(b), :] = drive_odd[b * t_win:(b + 1) * t_win, cs]

    n_k = n_slab // 2
    coef = [[(ar_ref[d, :, k * LANES:(k + 1) * LANES], ai_ref[d, :, k * LANES:(k + 1) * LANES])
             for k in range(n_k)] for d in range(N_DIR)]

    def step(it, carry):
        new = []
        for d in range(N_DIR):
            t = it if d == 0 else t_len - 1 - it
            rows = pl.ds(t, SUBLANES, stride=pitch)
            hs, hl = carry[d]
            nhs = [None] * n_slab
            for k in range(n_k):
                ar, ai = coef[d][k]
                hr, hi = hs[k], hs[n_k + k]
                nr = ar * hr - ai * hi + s5_in[d, k, rows, :]
                ni = ar * hi + ai * hr + s5_in[d, n_k + k, rows, :]
                s5_out[d, k, rows, :] = nr
                s5_out[d, n_k + k, rows, :] = ni
                nhs[k], nhs[n_k + k] = nr, ni
            nhl = lru_in[d, 0, rows, :] * hl + lru_in[d, 1, rows, :]
            lru_out[d, rows, :] = nhl
            new.append((tuple(nhs), nhl))
        return tuple(new)

    init = tuple((tuple(s5_state[d, sl] for sl in range(n_slab)), lru_state[d]) for d in range(N_DIR))
    fin = lax.fori_loop(0, t_len, step, init, unroll=4)
    for d in range(N_DIR):
        for sl in range(n_slab):
            s5_state[d, sl] = fin[d][0][sl]
        lru_state[d] = fin[d][1]

    for d, l_ref, y_ref in ((0, lf_ref, yf_ref), (1, lb_ref, yb_ref)):
        h_even = jnp.concatenate(
            [jnp.concatenate([s5_out[d, sl, even_rows(b), :] for sl in range(n_slab)], axis=1)
             for b in range(n_b)], axis=0)
        h_odd = jnp.concatenate(
            [jnp.concatenate([s5_out[d, sl, odd_rows(b), :] for sl in range(n_slab)], axis=1)
             for b in range(n_b)], axis=0)
        y_even = _dot(h_even.astype(_BF16), cm_ref[d, :half_w, :])
        y_odd = _dot(h_odd.astype(_BF16), cm_ref[d, half_w:, :])
        for b in range(n_b):
            y_ref[b] = (y_even[b * t_len:(b + 1) * t_len]
                        + pltpu.roll(y_odd[b * t_win:(b + 1) * t_win], t_win - off, 0)[:t_len])
            l_ref[b] = jnp.concatenate(
                [lru_out[d, even_rows(b), :],
                 pltpu.roll(lru_out[d, odd_rows(b), :], t_win - off, 0)[:t_len]], axis=1)


def _scans(xl, u, sp, n_lat, n_ctx):
    b, s, _ = xl.shape
    t = SCAN_CHUNK
    nl, nc = n_lat // t, n_ctx // t
    n_tot = nl + nc
    n8 = s // SUBLANES
    per8 = t // SUBLANES
    cf = lambda i: jnp.where(i < nc, nl + i, i - nc)
    cb = lambda i: n_tot - 1 - i
    main = lambda f: pl.BlockSpec((b, t, LRU_W), lambda i: (0, f(i), 0))
    prev = lambda f: pl.BlockSpec((b, SUBLANES, LRU_W), lambda i: (0, jnp.maximum(f(i) * per8 - 1, 0), 0))
    nxt = lambda f: pl.BlockSpec((b, SUBLANES, LRU_W), lambda i: (0, jnp.minimum((f(i) + 1) * per8, n8 - 1), 0))
    n_slab = 2 * S5_FLAT // (2 * LANES)
    n_rows = SUBLANES * SCAN_PITCH
    weights = [sp["conv_w"], sp["conv_b"], sp["wa"], sp["wx"], sp["ba"], sp["bx"], sp["nsp"],
               sp["bd"], sp["cm"], sp["ar"], sp["ai"]]
    out_sd = jax.ShapeDtypeStruct((b, s, LRU_W), _F32)
    return pl.pallas_call(
        functools.partial(_scan_kernel, n_lat_chunks=nl, n_ctx_chunks=nc),
        grid=(n_tot,),
        in_specs=[main(cf), prev(cf), nxt(cf), main(cf), main(cb), prev(cb), nxt(cb), main(cb)]
                 + [_const_spec(w.shape) for w in weights],
        out_specs=[main(cf), main(cb), main(cf), main(cb)],
        out_shape=[out_sd] * 4,
        scratch_shapes=[pltpu.VMEM((N_DIR, n_slab, n_rows, LANES), _F32),
                        pltpu.VMEM((N_DIR, n_slab, n_rows, LANES), _F32),
                        pltpu.VMEM((N_DIR, 2, n_rows, LANES), _F32),
                        pltpu.VMEM((N_DIR, n_rows, LANES), _F32),
                        pltpu.VMEM((N_DIR, n_slab, SUBLANES, LANES), _F32),
                        pltpu.VMEM((N_DIR, SUBLANES, LANES), _F32)],
        compiler_params=_cparams(1),
        name="scans",
    )(xl, xl, xl, u, xl, xl, xl, u, *weights)


def _merge_kernel(x_ref, c_ref, mod_ref, attn_ref, gate_ref, lf_ref, lb_ref, yf_ref, yb_ref, u_ref,
                  mg_ref, dskip_ref, wglu_ref, wout_ref, o_ref, *, n_lat_tiles):
    x = jnp.where(pl.program_id(1) >= n_lat_tiles, c_ref[...], x_ref[...])
    o_lru = jax.nn.gelu(gate_ref[...]) * (lf_ref[...] + lb_ref[...])
    y = u_ref[...] * dskip_ref[...] + yf_ref[...] + yb_ref[...]
    z = jax.nn.gelu(y)
    o_s5 = z * jax.nn.sigmoid(_dot(z.astype(_BF16), wglu_ref[...]))
    mg = mg_ref[...]
    parts = [_rms_rows(attn_ref[...].astype(_F32)) * mg[:, :ATTN_W],
             _rms_rows(o_lru) * mg[:, ATTN_W:ATTN_W + LRU_W],
             _rms_rows(o_s5) * mg[:, ATTN_W + LRU_W:]]
    cat = jnp.concatenate([p.astype(_BF16) for p in parts], axis=1)
    o_ref[...] = x + mod_ref[2:3, :] * _dot(cat, wout_ref[...])


def _merge(x_src, c_src, ctx_tile0, msel, attn, gate, lf, lb, yf, yb, u, mg_p, dskip, w_glu, w_out_p,
           n_lat_tiles, n_tiles):
    b, _, d = x_src.shape
    tm = TOKEN_TILE
    tok = lambda w: pl.BlockSpec((None, tm, w), lambda bi, i: (bi, i, 0))
    return pl.pallas_call(
        functools.partial(_merge_kernel, n_lat_tiles=n_lat_tiles),
        grid=(b, n_tiles),
        in_specs=[*_tile_specs(tm, d, n_lat_tiles, ctx_tile0),
                  pl.BlockSpec((None, None, 6, d), lambda bi, i: (bi, (i >= n_lat_tiles).astype(jnp.int32), 0, 0)),
                  tok(ATTN_W), tok(LRU_W), tok(LRU_W), tok(LRU_W), tok(S5_W), tok(S5_W), tok(S5_W),
                  _const_spec((1, d)), _const_spec((1, S5_W)),
                  _const_spec((S5_W, S5_W)), _const_spec((d, d))],
        out_specs=tok(d),
        out_shape=jax.ShapeDtypeStruct((b, n_tiles * tm, d), _F32),
        compiler_params=_cparams(2),
        name="merge",
    )(x_src, c_src, msel, attn, gate, lf, lb, yf, yb, u, mg_p, dskip, w_glu, w_out_p)


def _ffn_kernel(x_ref, xp_ref, xn_ref, mod_ref, g_ref, wup_ref, cw_ref, cb_ref, wd_ref,
                o_ref, act_ref, *, n_lat_tiles, n_tiles):
    i = pl.program_id(1)
    tm = x_ref.shape[0]
    n_e = tm + 2 * SUBLANES
    first = (i == 0) | (i == n_lat_tiles)
    last = (i == n_lat_tiles - 1) | (i == n_tiles - 1)
    x = x_ref[...]
    xe = jnp.concatenate([x, xn_ref[...], xp_ref[...]], axis=0)
    scale = g_ref[...] * (1.0 + mod_ref[4:5, :])
    rowe = lax.broadcasted_iota(jnp.int32, (n_e, 1), 0)
    pad_row = ((rowe >= jnp.where(first, tm + SUBLANES, n_e))
               | ((rowe >= tm) & (rowe < jnp.where(last, tm + SUBLANES, tm))))
    he = jnp.where(pad_row, 0.0, _rms_rows(xe) * scale + mod_ref[3:4, :]).astype(_BF16)
    hc = he[:tm]
    for f in range(D_FF // FF_CHUNK):
        cs = slice(f * FF_CHUNK, (f + 1) * FF_CHUNK)
        ge = _dot(he, wup_ref[:, cs])
        gv = _dot(hc, wup_ref[:, D_FF + f * FF_CHUNK:D_FF + (f + 1) * FF_CHUNK])
        conv = (cb_ref[:, cs] + cw_ref[1:2, cs] * ge[:tm]
                + cw_ref[0:1, cs] * pltpu.roll(ge, 1, 0)[:tm]
                + cw_ref[2:3, cs] * pltpu.roll(ge, n_e - 1, 0)[:tm])
        act_ref[:, cs] = (jax.nn.gelu(conv) * gv).astype(_BF16)
    o_ref[...] = x + mod_ref[5:6, :] * _dot(act_ref[...], wd_ref[...])


def _ffn(x1, msel, norm_g, w_up, cw, cb, wd, n_lat_tiles, n_tiles):
    b, s, d = x1.shape
    tm = TOKEN_TILE
    per8 = tm // SUBLANES
    n8 = s // SUBLANES
    tok = pl.BlockSpec((None, tm, d), lambda bi, i: (bi, i, 0))
    resident = lambda a: pl.BlockSpec(a.shape, lambda *_: (0,) * a.ndim, pipeline_mode=pl.Buffered(1))
    return pl.pallas_call(
        functools.partial(_ffn_kernel, n_lat_tiles=n_lat_tiles, n_tiles=n_tiles),
        grid=(b, n_tiles),
        in_specs=[tok,
                  pl.BlockSpec((None, SUBLANES, d), lambda bi, i: (bi, jnp.maximum(i * per8 - 1, 0), 0)),
                  pl.BlockSpec((None, SUBLANES, d), lambda bi, i: (bi, jnp.minimum((i + 1) * per8, n8 - 1), 0)),
                  pl.BlockSpec((None, None, 6, d), lambda bi, i: (bi, (i >= n_lat_tiles).astype(jnp.int32), 0, 0)),
                  _const_spec((1, d)),
                  resident(w_up), _const_spec(cw.shape), _const_spec(cb.shape), resident(wd)],
        out_specs=tok,
        out_shape=jax.ShapeDtypeStruct((b, n_tiles * tm, d), _F32),
        scratch_shapes=[pltpu.VMEM((tm, D_FF), _BF16)],
        compiler_params=_cparams(2),
        name="ffn",
    )(x1, x1, x1, msel, norm_g, w_up, cw, cb, wd)


def _rope_tables(n_lat, n_ctx):
    t = jnp.arange(n_lat)
    row = (t // GRID_W).astype(_F32)
    col = (t % GRID_W).astype(_F32)
    n_freq = HEAD_DIM // 4
    inv = ROPE_BASE ** (-jnp.arange(n_freq, dtype=_F32) / n_freq)
    ang_r = row[:, None] * inv
    ang_c = col[:, None] * inv
    cos_h = jnp.concatenate([jnp.cos(ang_r), jnp.cos(ang_r), jnp.cos(ang_c), jnp.cos(ang_c)], axis=1)
    sin_h = jnp.concatenate([-jnp.sin(ang_r), jnp.sin(ang_r), -jnp.sin(ang_c), jnp.sin(ang_c)], axis=1)
    cos_t = jnp.concatenate([jnp.tile(cos_h, (1, 2)), jnp.ones((n_ctx, LANES), _F32)], axis=0)
    sin_t = jnp.concatenate([jnp.tile(sin_h, (1, 2)), jnp.zeros((n_ctx, LANES), _F32)], axis=0)
    return cos_t, sin_t


def _q_column_order():
    idx = []
    for g in range(GQA_GROUP):
        for hk in range(N_KV_HEADS):
            h = hk * GQA_GROUP + g
            idx.extend(range(h * HEAD_DIM, (h + 1) * HEAD_DIM))
    return jnp.array(idx, jnp.int32)


def _block_diag(blocks):
    n, r, c = blocks.shape
    eye = jnp.eye(n, dtype=blocks.dtype)
    return (eye[:, None, :, None] * blocks[:, :, None, :]).reshape(n * r, n * c)


def _s5_params(lam_re, lam_im, log_step, b_re, b_im, c_re, c_im):
    lr = jnp.minimum(lam_re.astype(_F32), -1e-4)
    li = lam_im.astype(_F32)
    dt = jnp.exp(log_step.astype(_F32))[:, None]
    mag = jnp.exp(lr * dt)
    ab_re = mag * jnp.cos(li * dt)
    ab_im = mag * jnp.sin(li * dt)
    nr = ab_re - 1
    den = lr * lr + li * li
    cr = ((nr * lr + ab_im * li) / den)[..., None]
    ci = ((ab_im * lr - nr * li) / den)[..., None]
    br = b_re.astype(_F32)
    bi = b_im.astype(_F32)
    bb_re = cr * br - ci * bi
    bb_im = cr * bi + ci * br
    half = S5_FLAT // 2
    d_re = _block_diag(jnp.swapaxes(bb_re, 1, 2))
    d_im = _block_diag(jnp.swapaxes(bb_im, 1, 2))
    bd = jnp.concatenate([d_re[:, :half], d_im[:, :half], d_re[:, half:], d_im[:, half:]], axis=1)
    r_re = _block_diag(jnp.swapaxes(c_re.astype(_F32), 1, 2))
    r_im = -_block_diag(jnp.swapaxes(c_im.astype(_F32), 1, 2))
    cm = jnp.concatenate([r_re[:half], r_im[:half], r_re[half:], r_im[half:]], axis=0)
    a_re = jnp.tile(ab_re.reshape(2, half), (SUBLANES // 2, 1))
    a_im = jnp.tile(ab_im.reshape(2, half), (SUBLANES // 2, 1))
    return bd.astype(_BF16), cm.astype(_BF16), a_re, a_im


def _per_layer_dir(fn):
    return jax.vmap(jax.vmap(fn))


def kernel(x, c, ctx, c_ctx, w_mod, b_mod, norm1_g, norm2_g, w_in, q_norm_g, k_norm_g, attn_sink,
           lru_conv_w, lru_conv_b, lru_wa, lru_ba, lru_wx, lru_bx, lru_lambda,
           s5_lam_re, s5_lam_im, s5_log_step, s5_b_re, s5_b_im, s5_c_re, s5_c_im, s5_d, s5_w_glu,
           mix_g, w_out, ffn_w_up, ffn_conv_w, ffn_conv_b, ffn_w_down):
    n_b, n_lat, d = x.shape
    n_ctx = ctx.shape[1]
    depth = w_mod.shape[0]
    assert d == D_MODEL and n_b * 2 == SUBLANES
    assert n_lat % TOKEN_TILE == 0 and n_ctx % TOKEN_TILE == 0 and n_lat % n_ctx == 0
    n_lat_tiles = n_lat // TOKEN_TILE
    n_all_tiles = (n_lat + n_ctx) // TOKEN_TILE

    cvecs = jnp.concatenate([c, c_ctx[None], jnp.zeros((SUBLANES - n_b - 1, d), _F32)], axis=0)
    mod = _modulation(cvecs, w_mod, b_mod).reshape(depth, SUBLANES, 6, d)
    msel = jnp.stack([mod[:, :n_b], jnp.broadcast_to(mod[:, n_b:n_b + 1], (depth, n_b, 6, d))], axis=2)
    cos_t, sin_t = _rope_tables(n_lat, n_ctx)

    q_cols = _q_column_order()
    in_cols = jnp.concatenate([q_cols, jnp.arange(ATTN_W, IN_W, dtype=jnp.int32)])
    mix_rows = jnp.concatenate([q_cols, jnp.arange(ATTN_W, D_MODEL, dtype=jnp.int32)])
    w_in_p = w_in[:, :, in_cols].astype(_BF16)
    w_out_p = w_out[:, mix_rows].astype(_BF16)
    mix_g_p = mix_g[:, None, mix_rows]
    qg = jnp.tile(q_norm_g, (1, 2))[:, None] * (HEAD_DIM ** -0.5 * LOG2E)
    kg = jnp.tile(k_norm_g, (1, 2))[:, None]
    w_glu = s5_w_glu.astype(_BF16)
    w_up = ffn_w_up.astype(_BF16)
    w_down = ffn_w_down.astype(_BF16)
    bd, cm, a_re, a_im = _per_layer_dir(_s5_params)(s5_lam_re, s5_lam_im, s5_log_step, s5_b_re, s5_b_im,
                                                    s5_c_re, s5_c_im)
    wa = _per_layer_dir(_block_diag)(lru_wa).astype(_BF16)
    wx = _per_layer_dir(_block_diag)(lru_wx).astype(_BF16)
    nsp = -LRU_C * jax.nn.softplus(-lru_lambda.astype(_F32))

    x_src, c_src, ctx_tile0 = x, ctx, 0
    for l in range(depth):
        last = l == depth - 1
        q, kv, gate, xl, u = _inproj(x_src, c_src, ctx_tile0, msel[l], norm1_g[l][None], w_in_p[l], qg[l], kg[l],
                                     cos_t, sin_t, n_lat_tiles, n_all_tiles)
        attn = _attention(q, kv, attn_sink[l], n_lat, n_ctx, with_ctx_queries=not last)
        sp = {
            "conv_w": lru_conv_w[l], "conv_b": lru_conv_b[l][None], "wa": wa[l], "wx": wx[l],
            "ba": lru_ba[l][:, None, :], "bx": lru_bx[l][:, None, :], "nsp": nsp[l][:, None, :],
            "bd": bd[l], "cm": cm[l], "ar": a_re[l], "ai": a_im[l],
        }
        lf, lb, yf, yb = _scans(xl, u, sp, n_lat, n_ctx)
        n_tiles = n_lat_tiles if last else n_all_tiles
        x1 = _merge(x_src, c_src, ctx_tile0, msel[l], attn, gate, lf, lb, yf, yb, u, mix_g_p[l], s5_d[l][None],
                    w_glu[l], w_out_p[l], n_lat_tiles, n_tiles)
        x_all = _ffn(x1, msel[l], norm2_g[l][None], w_up[l], ffn_conv_w[l], ffn_conv_b[l][None], w_down[l],
                     n_lat_tiles, n_tiles)
        x_src, c_src, ctx_tile0 = x_all, x_all, n_lat_tiles
    return x_all
```

```python
import functools

import jax
import jax.numpy as jnp
from jax import lax
from jax.experimental import pallas as pl
from jax.experimental.pallas import tpu as pltpu

D_MODEL = 1024
HEAD_DIM = 64
N_Q_HEADS = 8
N_KV_HEADS = 2
GQA_GROUP = N_Q_HEADS // N_KV_HEADS
ATTN_W = N_Q_HEADS * HEAD_DIM
KV_W = N_KV_HEADS * HEAD_DIM
WINDOW = 128
BLOCK = 128
GRID_W = 64
ROPE_BASE = 10000.0
LRU_W = 256
LRU_HEADS = 4
LRU_HEAD_W = LRU_W // LRU_HEADS
LRU_C = 8.0
S5_W = 256
S5_GROUP = 16
S5_GROUPS = S5_W // S5_GROUP
S5_STATE = 64
S5_FLAT = S5_GROUPS * S5_STATE
N_DIR = 2
IN_W = ATTN_W + 2 * KV_W + 2 * LRU_W + S5_W
D_FF = 2816
NEG = -1e30
EPS = 1e-6
LOG2E = 1.4426950408889634

LANES = 128
SUBLANES = 8
TOKEN_TILE = 256
TILES_PER_STEP = 2
SCAN_CHUNK = 128
SCAN_PITCH = SCAN_CHUNK + SUBLANES // 2
FF_CHUNK = 256
VMEM_LIMIT = 56 * 1024 * 1024

_F32 = jnp.float32
_BF16 = jnp.bfloat16


def _cparams(n_axes):
    return pltpu.CompilerParams(dimension_semantics=("arbitrary",) * n_axes,
                                vmem_limit_bytes=VMEM_LIMIT)


def _const_spec(shape):
    nd = len(shape)
    return pl.BlockSpec(shape, lambda *_: (0,) * nd)


def _dot(a, b):
    return jnp.dot(a, b, preferred_element_type=_F32)


def _lane_iota(shape):
    return lax.broadcasted_iota(jnp.int32, shape, len(shape) - 1)


def _mod_kernel(c_ref, w_ref, b_ref, o_ref):
    a = jax.nn.silu(c_ref[...])
    o_ref[...] = jnp.dot(a, w_ref[...], preferred_element_type=_F32,
                         precision=lax.Precision.HIGHEST) + b_ref[...]


def _modulation(cvecs, w_mod, b_mod):
    depth, d, n = w_mod.shape
    nt = 1536
    return pl.pallas_call(
        _mod_kernel,
        grid=(depth, n // nt),
        in_specs=[pl.BlockSpec((SUBLANES, d), lambda l, j: (0, 0)),
                  pl.BlockSpec((None, d, nt), lambda l, j: (l, 0, j)),
                  pl.BlockSpec((None, 1, nt), lambda l, j: (l, 0, j))],
        out_specs=pl.BlockSpec((None, SUBLANES, nt), lambda l, j: (l, 0, j)),
        out_shape=jax.ShapeDtypeStruct((depth, SUBLANES, n), _F32),
        compiler_params=_cparams(2),
        name="modulation",
    )(cvecs, w_mod, b_mod.reshape(depth, 1, n))


def _rms_rows(x):
    return x * lax.rsqrt(jnp.mean(x * x, axis=-1, keepdims=True) + EPS)


def _head_norm_rope(t, gain, cos, sin):
    lane = _lane_iota(t.shape)
    low = lane < HEAD_DIM
    sq = t * t
    s_low = jnp.sum(jnp.where(low, sq, 0.0), axis=-1, keepdims=True)
    s_all = jnp.sum(sq, axis=-1, keepdims=True)
    ms = jnp.where(low, s_low, s_all - s_low) * (1.0 / HEAD_DIM)
    tn = t * lax.rsqrt(ms + EPS) * gain
    partner = jnp.where((lane % 32) < 16, pltpu.roll(tn, LANES - 16, 1), pltpu.roll(tn, 16, 1))
    return tn * cos + partner * sin


def _tile_specs(tm, d, n_lat_tiles, ctx_tile0):
    lat = pl.BlockSpec((None, tm, d), lambda bi, i: (bi, jnp.minimum(i, n_lat_tiles - 1), 0))
    ctx = pl.BlockSpec((None, tm, d), lambda bi, i: (bi, ctx_tile0 + jnp.maximum(i - n_lat_tiles, 0), 0))
    return lat, ctx


def _inproj_tile(is_ctx, rows, x_ref, c_ref, mod_ref, cos_ref, sin_ref, g_ref, w_ref, qg_ref, kg_ref,
                 q_ref, kv_ref, gate_ref, xl_ref, u_ref):
    x = jnp.where(is_ctx, c_ref[...], x_ref[...])
    scale = g_ref[...] * (1.0 + mod_ref[1:2, :])
    h = _rms_rows(x) * scale + mod_ref[0:1, :]
    proj = _dot(h.astype(_BF16), w_ref[...])
    cos = cos_ref[...]
    sin = sin_ref[...]
    for g in range(GQA_GROUP):
        blk = proj[:, g * LANES:(g + 1) * LANES]
        q_ref[rows, g * LANES:(g + 1) * LANES] = _head_norm_rope(blk, qg_ref[...], cos, sin).astype(_BF16)
    o = ATTN_W
    kv_ref[rows, :KV_W] = _head_norm_rope(proj[:, o:o + KV_W], kg_ref[...], cos, sin).astype(_BF16)
    o += KV_W
    kv_ref[rows, KV_W:] = proj[:, o:o + KV_W].astype(_BF16)
    o += KV_W
    gate_ref[rows, :] = proj[:, o:o + LRU_W].astype(_BF16)
    o += LRU_W
    xl_ref[rows, :] = proj[:, o:o + LRU_W]
    o += LRU_W
    u_ref[rows, :] = proj[:, o:o + S5_W]


def _inproj_kernel(*refs, n_lat_tiles, n_tiles):
    n_in = 5
    shared_and_outs = refs[TILES_PER_STEP * n_in:]
    tm = refs[0].shape[0]
    for k in range(TILES_PER_STEP):
        tile = pl.program_id(0) * TILES_PER_STEP + k
        is_ctx = (tile % n_tiles) >= n_lat_tiles
        _inproj_tile(is_ctx, slice(k * tm, (k + 1) * tm), *refs[k * n_in:(k + 1) * n_in], *shared_and_outs)


def _flat_tile(s, k, n_tiles):
    t = s * TILES_PER_STEP + k
    return t // n_tiles, t % n_tiles


def _inproj(x_src, c_src, ctx_tile0, msel, norm_g, w_in_p, qg, kg, cos_t, sin_t, n_lat_tiles, n_tiles):
    b, _, d = x_src.shape
    tm = TOKEN_TILE
    n_flat = b * n_tiles
    assert n_flat % TILES_PER_STEP == 0
    widths = [(ATTN_W, _BF16), (2 * KV_W, _BF16), (LRU_W, _BF16), (LRU_W, _F32), (S5_W, _F32)]
    in_specs, in_args = [], []
    for k in range(TILES_PER_STEP):
        bt = functools.partial(_flat_tile, k=k, n_tiles=n_tiles)
        in_specs += [
            pl.BlockSpec((None, tm, d), lambda s, bt=bt: (bt(s)[0], jnp.minimum(bt(s)[1], n_lat_tiles - 1), 0)),
            pl.BlockSpec((None, tm, d),
                         lambda s, bt=bt: (bt(s)[0], ctx_tile0 + jnp.maximum(bt(s)[1] - n_lat_tiles, 0), 0)),
            pl.BlockSpec((None, None, 6, d),
                         lambda s, bt=bt: (bt(s)[0], (bt(s)[1] >= n_lat_tiles).astype(jnp.int32), 0, 0)),
            pl.BlockSpec((tm, LANES), lambda s, bt=bt: (bt(s)[1], 0)),
            pl.BlockSpec((tm, LANES), lambda s, bt=bt: (bt(s)[1], 0))]
        in_args += [x_src, c_src, msel, cos_t, sin_t]
    in_specs += [_const_spec((1, d)), _const_spec((d, IN_W)), _const_spec((1, LANES)), _const_spec((1, LANES))]
    res = pl.pallas_call(
        functools.partial(_inproj_kernel, n_lat_tiles=n_lat_tiles, n_tiles=n_tiles),
        grid=(n_flat // TILES_PER_STEP,),
        in_specs=in_specs,
        out_specs=[pl.BlockSpec((TILES_PER_STEP * tm, w), lambda s: (s, 0)) for w, _ in widths],
        out_shape=[jax.ShapeDtypeStruct((n_flat * tm, w), dt) for w, dt in widths],
        compiler_params=_cparams(1),
        name="inproj",
    )(*in_args, norm_g, w_in_p, qg, kg)
    return [r.reshape(b, n_tiles * tm, r.shape[-1]) for r in res]


def _attend(q_blk, pieces, sink_ref):
    n_slab = N_KV_HEADS * GQA_GROUP
    lane = _lane_iota((BLOCK, LANES))
    low = lane < HEAD_DIM
    zero = jnp.zeros((BLOCK, LANES), _BF16)
    slabs = [[None] * GQA_GROUP for _ in range(N_KV_HEADS)]
    for g in range(GQA_GROUP):
        blk = q_blk[:, g * LANES:(g + 1) * LANES]
        slabs[0][g] = jnp.where(low, blk, zero)
        slabs[1][g] = jnp.where(low, zero, blk)
    qst = jnp.concatenate(slabs[0] + slabs[1], axis=0)
    half = GQA_GROUP * BLOCK
    m = jnp.concatenate([jnp.full((BLOCK, LANES), sink_ref[s] * LOG2E, _F32) for s in range(n_slab)], axis=0)
    lane_h = _lane_iota((half, LANES)) < HEAD_DIM
    acc = [jnp.where(lane_h, 0.0, 1.0), jnp.where(lane_h, 1.0, 0.0)]
    for t0 in range(0, len(pieces), 2):
        tile = pieces[t0:t0 + 2]
        keys = jnp.concatenate([p[0] for p in tile], axis=0)
        vals = jnp.concatenate([p[1] for p in tile], axis=0)
        s = lax.dot_general(qst, keys, (((1,), (1,)), ((), ())), preferred_element_type=_F32)
        if any(p[2] is not None for p in tile):
            bias = jnp.concatenate([p[2] if p[2] is not None else jnp.zeros((BLOCK, BLOCK), _F32) for p in tile],
                                   axis=1)
            s = jnp.concatenate([s[i * BLOCK:(i + 1) * BLOCK] + bias for i in range(n_slab)], axis=0)
        m_new = jnp.maximum(m, jnp.broadcast_to(jnp.max(s, axis=-1, keepdims=True), m.shape))
        alpha = jnp.exp2(m - m_new)
        p = jnp.exp2(s - jnp.concatenate([m_new] * (s.shape[1] // LANES), axis=1)).astype(_BF16)
        m = m_new
        low_v = _lane_iota(vals.shape) < HEAD_DIM
        one = jnp.ones(vals.shape, _BF16)
        vext = (jnp.where(low_v, vals, one), jnp.where(low_v, one, vals))
        for hk in range(N_KV_HEADS):
            rs = slice(hk * half, (hk + 1) * half)
            acc[hk] = alpha[rs] * acc[hk] + _dot(p[rs], vext[hk])
    outs = []
    for g in range(GQA_GROUP):
        a0 = acc[0][g * BLOCK:(g + 1) * BLOCK]
        a1 = acc[1][g * BLOCK:(g + 1) * BLOCK]
        num = jnp.where(low, a0, a1)
        den = pltpu.roll(jnp.where(low, a1, a0), HEAD_DIM, 1)
        outs.append((num / den).astype(_BF16))
    return outs


def _attn_kernel(sink_ref, q_ref, kvp_ref, kvc_ref, kvn_ref, kvx_ref, o_ref, *, n_lat_pairs):
    qi = pl.program_id(1)
    is_lat = qi < n_lat_pairs
    row = lax.broadcasted_iota(jnp.int32, (BLOCK, BLOCK), 0)
    col = lax.broadcasted_iota(jnp.int32, (BLOCK, BLOCK), 1)
    open_ = jnp.where(is_lat, 0.0, NEG)
    tri_prev = jnp.where(col >= row, open_, NEG)
    tri_next = jnp.where(col <= row, open_, NEG)
    full = jnp.full((BLOCK, BLOCK), open_, _F32)
    shut = jnp.full((BLOCK, BLOCK), NEG, _F32)
    has_prev = is_lat & (qi > 0)
    has_next = qi < n_lat_pairs - 1

    def kv(ref, r0=0):
        return ref[r0:r0 + BLOCK, :KV_W], ref[r0:r0 + BLOCK, KV_W:]

    ctx = [kv(kvx_ref, c * BLOCK) + (None,) for c in range(kvx_ref.shape[0] // BLOCK)]
    lo, hi = kv(kvc_ref, 0), kv(kvc_ref, BLOCK)
    first = [kv(kvp_ref) + (jnp.where(has_prev, tri_prev, shut),), lo + (full,), hi + (tri_next,)] + ctx
    second = [lo + (tri_prev,), hi + (full,), kv(kvn_ref) + (jnp.where(has_next, tri_next, shut),)] + ctx
    for half, pieces in enumerate((first, second)):
        outs = _attend(q_ref[half * BLOCK:(half + 1) * BLOCK, :], pieces, sink_ref)
        for g in range(GQA_GROUP):
            o_ref[half * BLOCK:(half + 1) * BLOCK, g * LANES:(g + 1) * LANES] = outs[g]


def _attention(q, kv, sink, n_lat, n_ctx, with_ctx_queries):
    b, s, _ = q.shape
    pair = 2 * BLOCK
    nlp = n_lat // pair
    nq = s // pair if with_ctx_queries else nlp
    last = s // BLOCK - 1
    halo = lambda f: pl.BlockSpec((None, BLOCK, 2 * KV_W), lambda bi, i: (bi, f(i), 0))
    return pl.pallas_call(
        functools.partial(_attn_kernel, n_lat_pairs=nlp),
        grid=(b, nq),
        in_specs=[pl.BlockSpec(memory_space=pltpu.SMEM),
                  pl.BlockSpec((None, pair, ATTN_W), lambda bi, i: (bi, i, 0)),
                  halo(lambda i: jnp.maximum(2 * i - 1, 0)),
                  pl.BlockSpec((None, pair, 2 * KV_W), lambda bi, i: (bi, i, 0)),
                  halo(lambda i: jnp.minimum(2 * i + 2, last)),
                  pl.BlockSpec((None, n_ctx, 2 * KV_W), lambda bi, i: (bi, n_lat // n_ctx, 0))],
        out_specs=pl.BlockSpec((None, pair, ATTN_W), lambda bi, i: (bi, i, 0)),
        out_shape=jax.ShapeDtypeStruct((b, nq * pair, ATTN_W), _BF16),
        compiler_params=_cparams(2),
        name="attention",
    )(sink, q, kv, kv, kv, kv)


def _shift_rows(x, n_rows_out, shift):
    pad = jnp.zeros((n_rows_out - x.shape[0], x.shape[1]), x.dtype)
    return pltpu.roll(jnp.concatenate([x, pad], axis=0), shift, 0)


def _scan_kernel(xf_ref, xfp_ref, xfn_ref, uf_ref, xb_ref, xbp_ref, xbn_ref, ub_ref,
                 cw_ref, cb_ref, wa_ref, wx_ref, ba_ref, bx_ref, nsp_ref,
                 bd_ref, cm_ref, ar_ref, ai_ref,
                 of_ref, ob_ref,
                 s5_in, s5_out, lru_in, lru_out, s5_state, lru_state, *, n_lat_chunks, n_ctx_chunks):
    i = pl.program_id(0)
    n_b = xf_ref.shape[0]
    t_len = SCAN_CHUNK
    pitch = SCAN_PITCH
    off = pitch - t_len
    t_win = t_len + SUBLANES
    n_slab = 2 * S5_FLAT // (2 * LANES)
    half_w = n_slab * LANES
    n_tot = n_lat_chunks + n_ctx_chunks
    ic = jnp.minimum(i, n_tot - 1)
    chunk_f = jnp.where(ic < n_ctx_chunks, n_lat_chunks + ic, ic - n_ctx_chunks)
    chunk_b = n_tot - 1 - ic

    @pl.when(i == 0)
    def _():
        s5_state[...] = jnp.zeros_like(s5_state)
        lru_state[...] = jnp.zeros_like(lru_state)
        s5_out[...] = jnp.zeros_like(s5_out)
        lru_out[...] = jnp.zeros_like(lru_out)

    def seg_first(c):
        return (c == 0) | (c == n_lat_chunks)

    def seg_last(c):
        return (c == n_lat_chunks - 1) | (c == n_tot - 1)

    def even_rows(b):
        return pl.ds(2 * b * pitch, t_len)

    def odd_rows(b):
        return pl.ds((2 * b + 1) * pitch - off, t_win)

    for d, o_ref in ((0, of_ref), (1, ob_ref)):
        h_even = jnp.concatenate(
            [jnp.concatenate([s5_out[d, sl, even_rows(b), :] for sl in range(n_slab)], axis=1)
             for b in range(n_b)], axis=0)
        h_odd = jnp.concatenate(
            [jnp.concatenate([s5_out[d, sl, odd_rows(b), :] for sl in range(n_slab)], axis=1)
             for b in range(n_b)], axis=0)
        y_even = _dot(h_even.astype(_BF16), cm_ref[d, :half_w, :])
        y_odd = _dot(h_odd.astype(_BF16), cm_ref[d, half_w:, :])
        for b in range(n_b):
            y = (y_even[b * t_len:(b + 1) * t_len]
                 + pltpu.roll(y_odd[b * t_win:(b + 1) * t_win], t_win - off, 0)[:t_len])
            o_ref[b] = jnp.concatenate(
                [lru_out[d, even_rows(b), :],
                 pltpu.roll(lru_out[d, odd_rows(b), :], t_win - off, 0)[:t_len], y], axis=1).astype(_BF16)

    dirs = ((0, xf_ref, xfp_ref, xfn_ref, uf_ref, chunk_f), (1, xb_ref, xbp_ref, xbn_ref, ub_ref, chunk_b))
    for d, x_ref, xp_ref, xn_ref, u_ref, chunk in dirs:
        keep_prev = jnp.where(seg_first(chunk), 0.0, 1.0)
        keep_next = jnp.where(seg_last(chunk), 0.0, 1.0)
        n_e = t_len + 2 * SUBLANES
        xcs = []
        for b in range(n_b):
            xe = jnp.concatenate([x_ref[b], xn_ref[b] * keep_next, xp_ref[b] * keep_prev], axis=0)
            acc = cb_ref[...] + cw_ref[1:2, :] * x_ref[b]
            for j, shift in ((0, 1), (2, n_e - 1), (3, n_e - 2)):
                acc = acc + cw_ref[j:j + 1, :] * pltpu.roll(xe, shift, 0)[:t_len]
            xcs.append(acc)
        xc = jnp.concatenate(xcs, axis=0)
        xcb = xc.astype(_BF16)
        r = jax.nn.sigmoid(_dot(xcb, wa_ref[d]) + ba_ref[d])
        gi = jax.nn.sigmoid(_dot(xcb, wx_ref[d]) + bx_ref[d])
        log_a = nsp_ref[d] * r
        a = jnp.exp(log_a)
        bco = jnp.sqrt(-jnp.tanh(log_a) * (a * a + 1.0)) * gi * xc
        for b in range(n_b):
            rb = slice(b * t_len, (b + 1) * t_len)
            lru_in[d, 0, even_rows(b), :] = a[rb, :LANES]
            lru_in[d, 1, even_rows(b), :] = bco[rb, :LANES]
            lru_in[d, 0, odd_rows(b), :] = _shift_rows(a[rb, LANES:], t_win, off)
            lru_in[d, 1, odd_rows(b), :] = _shift_rows(bco[rb, LANES:], t_win, off)
        u_even = jnp.concatenate([u_ref[b] for b in range(n_b)], axis=0)
        u_odd = jnp.concatenate([_shift_rows(u_ref[b], t_win, off) for b in range(n_b)], axis=0)
        drive_even = _dot(u_even.astype(_BF16), bd_ref[d, :, :half_w])
        drive_odd = _dot(u_odd.astype(_BF16), bd_ref[d, :, half_w:])
        for b in range(n_b):
            for sl in range(n_slab):
                cs = slice(sl * LANES, (sl + 1) * LANES)
                s5_in[d, sl, even_rows(b), :] = drive_even[b * t_len:(b + 1) * t_len, cs]
                s5_in[d, sl, odd_rows(b), :] = drive_odd[b * t_win:(b + 1) * t_win, cs]

    n_k = n_slab // 2
    coef = [[(ar_ref[d, :, k * LANES:(k + 1) * LANES], ai_ref[d, :, k * LANES:(k + 1) * LANES])
             for k in range(n_k)] for d in range(N_DIR)]

    def step(it, carry):
        new = []
        for d in range(N_DIR):
            t = it if d == 0 else t_len - 1 - it
            rows = pl.ds(t, SUBLANES, stride=pitch)
            hs, hl = carry[d]
            nhs = [None] * n_slab
            for k in range(n_k):
                ar, ai = coef[d][k]
                hr, hi = hs[k], hs[n_k + k]
                nr = ar * hr - ai * hi + s5_in[d, k, rows, :]
                ni = ar * hi + ai * hr + s5_in[d, n_k + k, rows, :]
                s5_out[d, k, rows, :] = nr
                s5_out[d, n_k + k, rows, :] = ni
                nhs[k], nhs[n_k + k] = nr, ni
            nhl = lru_in[d, 0, rows, :] * hl + lru_in[d, 1, rows, :]
            lru_out[d, rows, :] = nhl
            new.append((tuple(nhs), nhl))
        return tuple(new)

    @pl.when(i < n_tot)
    def _():
        init = tuple((tuple(s5_state[d, sl] for sl in range(n_slab)), lru_state[d]) for d in range(N_DIR))
        fin = lax.fori_loop(0, t_len, step, init, unroll=4)
        for d in range(N_DIR):
            for sl in range(n_slab):
                s5_state[d, sl] = fin[d][0][sl]
            lru_state[d] = fin[d][1]


def _scans(xl, u, sp, n_lat, n_ctx):
    b, s, _ = xl.shape
    t = SCAN_CHUNK
    nl, nc = n_lat // t, n_ctx // t
    n_tot = nl + nc
    n8 = s // SUBLANES
    per8 = t // SUBLANES
    cf = lambda i: jnp.where(i < nc, nl + i, i - nc)
    cb = lambda i: n_tot - 1 - i
    cur = lambda i: jnp.minimum(i, n_tot - 1)
    done = lambda i: jnp.maximum(i - 1, 0)
    main = lambda f: pl.BlockSpec((b, t, LRU_W), lambda i: (0, f(cur(i)), 0))
    outb = lambda f: pl.BlockSpec((b, t, LRU_W + S5_W), lambda i: (0, f(done(i)), 0))
    prev = lambda f: pl.BlockSpec((b, SUBLANES, LRU_W), lambda i: (0, jnp.maximum(f(cur(i)) * per8 - 1, 0), 0))
    nxt = lambda f: pl.BlockSpec((b, SUBLANES, LRU_W),
                                 lambda i: (0, jnp.minimum((f(cur(i)) + 1) * per8, n8 - 1), 0))
    n_slab = 2 * S5_FLAT // (2 * LANES)
    n_rows = SUBLANES * SCAN_PITCH
    weights = [sp["conv_w"], sp["conv_b"], sp["wa"], sp["wx"], sp["ba"], sp["bx"], sp["nsp"],
               sp["bd"], sp["cm"], sp["ar"], sp["ai"]]
    out_sd = jax.ShapeDtypeStruct((b, s, LRU_W + S5_W), _BF16)
    return pl.pallas_call(
        functools.partial(_scan_kernel, n_lat_chunks=nl, n_ctx_chunks=nc),
        grid=(n_tot + 1,),
        in_specs=[main(cf), prev(cf), nxt(cf), main(cf), main(cb), prev(cb), nxt(cb), main(cb)]
                 + [_const_spec(w.shape) for w in weights],
        out_specs=[outb(cf), outb(cb)],
        out_shape=[out_sd] * 2,
        scratch_shapes=[pltpu.VMEM((N_DIR, n_slab, n_rows, LANES), _F32),
                        pltpu.VMEM((N_DIR, n_slab, n_rows, LANES), _F32),
                        pltpu.VMEM((N_DIR, 2, n_rows, LANES), _F32),
                        pltpu.VMEM((N_DIR, n_rows, LANES), _F32),
                        pltpu.VMEM((N_DIR, n_slab, SUBLANES, LANES), _F32),
                        pltpu.VMEM((N_DIR, SUBLANES, LANES), _F32)],
        compiler_params=_cparams(1),
        name="scans",
    )(xl, xl, xl, u, xl, xl, xl, u, *weights)


def _merge_kernel(x_ref, c_ref, mod_ref, attn_ref, gate_ref, of_ref, ob_ref, u_ref,
                  mg_ref, dskip_ref, wglu_ref, wout_ref, o_ref, *, n_lat_tiles):
    x = jnp.where(pl.program_id(1) >= n_lat_tiles, c_ref[...], x_ref[...])
    both = of_ref[...].astype(_F32) + ob_ref[...].astype(_F32)
    o_lru = jax.nn.gelu(gate_ref[...].astype(_F32)) * both[:, :LRU_W]
    y = u_ref[...] * dskip_ref[...] + both[:, LRU_W:]
    z = jax.nn.gelu(y)
    o_s5 = z * jax.nn.sigmoid(_dot(z.astype(_BF16), wglu_ref[...]))
    mg = mg_ref[...]
    parts = [_rms_rows(attn_ref[...].astype(_F32)) * mg[:, :ATTN_W],
             _rms_rows(o_lru) * mg[:, ATTN_W:ATTN_W + LRU_W],
             _rms_rows(o_s5) * mg[:, ATTN_W + LRU_W:]]
    cat = jnp.concatenate([p.astype(_BF16) for p in parts], axis=1)
    o_ref[...] = x + mod_ref[2:3, :] * _dot(cat, wout_ref[...])


def _merge(x_src, c_src, ctx_tile0, msel, attn, gate, of, ob, u, mg_p, dskip, w_glu, w_out_p,
           n_lat_tiles, n_tiles):
    b, _, d = x_src.shape
    tm = TOKEN_TILE
    tok = lambda w: pl.BlockSpec((None, tm, w), lambda bi, i: (bi, i, 0))
    return pl.pallas_call(
        functools.partial(_merge_kernel, n_lat_tiles=n_lat_tiles),
        grid=(b, n_tiles),
        in_specs=[*_tile_specs(tm, d, n_lat_tiles, ctx_tile0),
                  pl.BlockSpec((None, None, 6, d), lambda bi, i: (bi, (i >= n_lat_tiles).astype(jnp.int32), 0, 0)),
                  tok(ATTN_W), tok(LRU_W), tok(LRU_W + S5_W), tok(LRU_W + S5_W), tok(S5_W),
                  _const_spec((1, d)), _const_spec((1, S5_W)),
                  _const_spec((S5_W, S5_W)), _const_spec((d, d))],
        out_specs=tok(d),
        out_shape=jax.ShapeDtypeStruct((b, n_tiles * tm, d), _F32),
        compiler_params=_cparams(2),
        name="merge",
    )(x_src, c_src, msel, attn, gate, of, ob, u, mg_p, dskip, w_glu, w_out_p)


def _ffn_tile(i, rows, x_ref, xp_ref, xn_ref, mod_ref, g_ref, wup_ref, cw_ref, cb_ref, wd_ref, o_ref, act_ref,
              n_lat_tiles, n_tiles):
    tm = x_ref.shape[0]
    n_e = tm + 2 * SUBLANES
    first = (i == 0) | (i == n_lat_tiles)
    last = (i == n_lat_tiles - 1) | (i == n_tiles - 1)
    x = x_ref[...]
    xe = jnp.concatenate([x, xn_ref[...], xp_ref[...]], axis=0)
    scale = g_ref[...] * (1.0 + mod_ref[4:5, :])
    rowe = lax.broadcasted_iota(jnp.int32, (n_e, 1), 0)
    pad_row = ((rowe >= jnp.where(first, tm + SUBLANES, n_e))
               | ((rowe >= tm) & (rowe < jnp.where(last, tm + SUBLANES, tm))))
    he = jnp.where(pad_row, 0.0, _rms_rows(xe) * scale + mod_ref[3:4, :]).astype(_BF16)
    hc = he[:tm]
    for f in range(D_FF // FF_CHUNK):
        cs = slice(f * FF_CHUNK, (f + 1) * FF_CHUNK)
        ge = _dot(he, wup_ref[:, cs])
        gv = _dot(hc, wup_ref[:, D_FF + f * FF_CHUNK:D_FF + (f + 1) * FF_CHUNK])
        conv = (cb_ref[:, cs] + cw_ref[1:2, cs] * ge[:tm]
                + cw_ref[0:1, cs] * pltpu.roll(ge, 1, 0)[:tm]
                + cw_ref[2:3, cs] * pltpu.roll(ge, n_e - 1, 0)[:tm])
        act_ref[:, cs] = (jax.nn.gelu(conv) * gv).astype(_BF16)
    o_ref[rows, :] = x + mod_ref[5:6, :] * _dot(act_ref[...], wd_ref[...])


def _ffn_kernel(*refs, n_lat_tiles, n_tiles):
    n_in = 4
    g_ref, wup_ref, cw_ref, cb_ref, wd_ref, o_ref, act_ref = refs[TILES_PER_STEP * n_in:]
    tm = refs[0].shape[0]
    for k in range(TILES_PER_STEP):
        tile = pl.program_id(0) * TILES_PER_STEP + k
        _ffn_tile(tile % n_tiles, slice(k * tm, (k + 1) * tm), *refs[k * n_in:(k + 1) * n_in],
                  g_ref, wup_ref, cw_ref, cb_ref, wd_ref, o_ref, act_ref.at[k], n_lat_tiles, n_tiles)


def _ffn(x1, msel, norm_g, w_up, cw, cb, wd, n_lat_tiles, n_tiles):
    b, s, d = x1.shape
    tm = TOKEN_TILE
    per8 = tm // SUBLANES
    n_flat = b * n_tiles
    assert s == n_tiles * tm and n_flat % TILES_PER_STEP == 0
    x1f = x1.reshape(b * s, d)
    n8 = b * s // SUBLANES
    resident = lambda a: pl.BlockSpec(a.shape, lambda *_: (0,) * a.ndim, pipeline_mode=pl.Buffered(1))
    in_specs, in_args = [], []
    for k in range(TILES_PER_STEP):
        ft = lambda st, k=k: st * TILES_PER_STEP + k
        in_specs += [
            pl.BlockSpec((tm, d), lambda st, ft=ft: (ft(st), 0)),
            pl.BlockSpec((SUBLANES, d), lambda st, ft=ft: (jnp.maximum(ft(st) * per8 - 1, 0), 0)),
            pl.BlockSpec((SUBLANES, d), lambda st, ft=ft: (jnp.minimum((ft(st) + 1) * per8, n8 - 1), 0)),
            pl.BlockSpec((None, None, 6, d),
                         lambda st, ft=ft: (ft(st) // n_tiles, (ft(st) % n_tiles >= n_lat_tiles).astype(jnp.int32),
                                            0, 0))]
        in_args += [x1f, x1f, x1f, msel]
    in_specs += [_const_spec((1, d)), resident(w_up), _const_spec(cw.shape), _const_spec(cb.shape), resident(wd)]
    out = pl.pallas_call(
        functools.partial(_ffn_kernel, n_lat_tiles=n_lat_tiles, n_tiles=n_tiles),
        grid=(n_flat // TILES_PER_STEP,),
        in_specs=in_specs,
        out_specs=pl.BlockSpec((TILES_PER_STEP * tm, d), lambda st: (st, 0)),
        out_shape=jax.ShapeDtypeStruct((b * s, d), _F32),
        scratch_shapes=[pltpu.VMEM((TILES_PER_STEP, tm, D_FF), _BF16)],
        compiler_params=_cparams(1),
        name="ffn",
    )(*in_args, norm_g, w_up, cw, cb, wd)
    return out.reshape(b, s, d)


def _rope_tables(n_lat, n_ctx):
    t = jnp.arange(n_lat)
    row = (t // GRID_W).astype(_F32)
    col = (t % GRID_W).astype(_F32)
    n_freq = HEAD_DIM // 4
    inv = ROPE_BASE ** (-jnp.arange(n_freq, dtype=_F32) / n_freq)
    ang_r = row[:, None] * inv
    ang_c = col[:, None] * inv
    cos_h = jnp.concatenate([jnp.cos(ang_r), jnp.cos(ang_r), jnp.cos(ang_c), jnp.cos(ang_c)], axis=1)
    sin_h = jnp.concatenate([-jnp.sin(ang_r), jnp.sin(ang_r), -jnp.sin(ang_c), jnp.sin(ang_c)], axis=1)
    cos_t = jnp.concatenate([jnp.tile(cos_h, (1, 2)), jnp.ones((n_ctx, LANES), _F32)], axis=0)
    sin_t = jnp.concatenate([jnp.tile(sin_h, (1, 2)), jnp.zeros((n_ctx, LANES), _F32)], axis=0)
    return cos_t, sin_t


def _heads_kv_minor(a, axis):
    shape = a.shape
    a = a.reshape(shape[:axis] + (N_KV_HEADS, GQA_GROUP, HEAD_DIM) + shape[axis + 1:])
    return jnp.swapaxes(a, axis, axis + 1).reshape(shape)


def _block_diag(blocks):
    n, r, c = blocks.shape
    tiled = jnp.tile(blocks.reshape(n * r, c), (1, n))
    row = lax.broadcasted_iota(jnp.int32, (n * r, n * c), 0) // r
    col = lax.broadcasted_iota(jnp.int32, (n * r, n * c), 1) // c
    return jnp.where(row == col, tiled, 0.0)


def _s5_params(lam_re, lam_im, log_step, b_re, b_im, c_re, c_im):
    lr = jnp.minimum(lam_re.astype(_F32), -1e-4)
    li = lam_im.astype(_F32)
    dt = jnp.exp(log_step.astype(_F32))[:, None]
    mag = jnp.exp(lr * dt)
    ab_re = mag * jnp.cos(li * dt)
    ab_im = mag * jnp.sin(li * dt)
    nr = ab_re - 1
    den = lr * lr + li * li
    cr = ((nr * lr + ab_im * li) / den)[..., None]
    ci = ((ab_im * lr - nr * li) / den)[..., None]
    br = b_re.astype(_F32)
    bi = b_im.astype(_F32)
    bb_re = cr * br - ci * bi
    bb_im = cr * bi + ci * br
    half = S5_FLAT // 2
    d_re = _block_diag(jnp.swapaxes(bb_re, 1, 2))
    d_im = _block_diag(jnp.swapaxes(bb_im, 1, 2))
    bd = jnp.concatenate([d_re[:, :half], d_im[:, :half], d_re[:, half:], d_im[:, half:]], axis=1)
    r_re = _block_diag(jnp.swapaxes(c_re.astype(_F32), 1, 2))
    r_im = -_block_diag(jnp.swapaxes(c_im.astype(_F32), 1, 2))
    cm = jnp.concatenate([r_re[:half], r_im[:half], r_re[half:], r_im[half:]], axis=0)
    a_re = jnp.tile(ab_re.reshape(2, half), (SUBLANES // 2, 1))
    a_im = jnp.tile(ab_im.reshape(2, half), (SUBLANES // 2, 1))
    return bd.astype(_BF16), cm.astype(_BF16), a_re, a_im


def _per_dir(fn):
    return jax.vmap(fn)


def kernel(x, c, ctx, c_ctx, w_mod, b_mod, norm1_g, norm2_g, w_in, q_norm_g, k_norm_g, attn_sink,
           lru_conv_w, lru_conv_b, lru_wa, lru_ba, lru_wx, lru_bx, lru_lambda,
           s5_lam_re, s5_lam_im, s5_log_step, s5_b_re, s5_b_im, s5_c_re, s5_c_im, s5_d, s5_w_glu,
           mix_g, w_out, ffn_w_up, ffn_conv_w, ffn_conv_b, ffn_w_down):
    n_b, n_lat, d = x.shape
    n_ctx = ctx.shape[1]
    depth = w_mod.shape[0]
    assert d == D_MODEL and n_b * 2 == SUBLANES
    assert n_lat % TOKEN_TILE == 0 and n_ctx % TOKEN_TILE == 0 and n_lat % n_ctx == 0
    n_lat_tiles = n_lat // TOKEN_TILE
    n_all_tiles = (n_lat + n_ctx) // TOKEN_TILE

    cvecs = jnp.concatenate([c, c_ctx[None], jnp.zeros((SUBLANES - n_b - 1, d), _F32)], axis=0)
    mod = _modulation(cvecs, w_mod, b_mod).reshape(depth, SUBLANES, 6, d)
    msel = jnp.stack([mod[:, :n_b], jnp.broadcast_to(mod[:, n_b:n_b + 1], (depth, n_b, 6, d))], axis=2)
    cos_t, sin_t = _rope_tables(n_lat, n_ctx)
    qg = jnp.tile(q_norm_g, (1, 2))[:, None] * (HEAD_DIM ** -0.5 * LOG2E)
    kg = jnp.tile(k_norm_g, (1, 2))[:, None]
    nsp = -LRU_C * jax.nn.softplus(-lru_lambda.astype(_F32))

    x_src, c_src, ctx_tile0 = x, ctx, 0
    for l in range(depth):
        last = l == depth - 1
        w_in_p = jnp.concatenate([_heads_kv_minor(w_in[l][:, :ATTN_W], 1), w_in[l][:, ATTN_W:]], axis=1).astype(_BF16)
        w_out_p = jnp.concatenate([_heads_kv_minor(w_out[l][:ATTN_W], 0), w_out[l][ATTN_W:]], axis=0).astype(_BF16)
        mix_g_p = jnp.concatenate([_heads_kv_minor(mix_g[l][:ATTN_W], 0), mix_g[l][ATTN_W:]])[None]
        bd, cm, a_re, a_im = _per_dir(_s5_params)(s5_lam_re[l], s5_lam_im[l], s5_log_step[l], s5_b_re[l],
                                                  s5_b_im[l], s5_c_re[l], s5_c_im[l])
        sp = {
            "conv_w": lru_conv_w[l], "conv_b": lru_conv_b[l][None],
            "wa": _per_dir(_block_diag)(lru_wa[l]).astype(_BF16), "wx": _per_dir(_block_diag)(lru_wx[l]).astype(_BF16),
            "ba": lru_ba[l][:, None, :], "bx": lru_bx[l][:, None, :], "nsp": nsp[l][:, None, :],
            "bd": bd, "cm": cm, "ar": a_re, "ai": a_im,
        }

        q, kv, gate, xl, u = _inproj(x_src, c_src, ctx_tile0, msel[l], norm1_g[l][None], w_in_p, qg[l], kg[l],
                                     cos_t, sin_t, n_lat_tiles, n_all_tiles)
        attn = _attention(q, kv, attn_sink[l], n_lat, n_ctx, with_ctx_queries=not last)
        of, ob = _scans(xl, u, sp, n_lat, n_ctx)
        n_tiles = n_lat_tiles if last else n_all_tiles
        x1 = _merge(x_src, c_src, ctx_tile0, msel[l], attn, gate, of, ob, u, mix_g_p, s5_d[l][None],
                    s5_w_glu[l].astype(_BF16), w_out_p, n_lat_tiles, n_tiles)
        x_all = _ffn(x1, msel[l], norm2_g[l][None], ffn_w_up[l].astype(_BF16), ffn_conv_w[l], ffn_conv_b[l][None],
                     ffn_w_down[l].astype(_BF16), n_lat_tiles, n_tiles)
        x_src, c_src, ctx_tile0 = x_all, x_all, n_lat_tiles
    return x_all
```

```python
import functools

import jax
import jax.numpy as jnp
from jax import lax
from jax.experimental import pallas as pl
from jax.experimental.pallas import tpu as pltpu

D_MODEL = 1024
HEAD_DIM = 64
N_Q_HEADS = 8
N_KV_HEADS = 2
GQA_GROUP = N_Q_HEADS // N_KV_HEADS
ATTN_W = N_Q_HEADS * HEAD_DIM
KV_W = N_KV_HEADS * HEAD_DIM
WINDOW = 128
BLOCK = 128
GRID_W = 64
ROPE_BASE = 10000.0
LRU_W = 256
LRU_HEADS = 4
LRU_HEAD_W = LRU_W // LRU_HEADS
LRU_C = 8.0
S5_W = 256
S5_GROUP = 16
S5_GROUPS = S5_W // S5_GROUP
S5_STATE = 64
S5_FLAT = S5_GROUPS * S5_STATE
N_DIR = 2
IN_W = ATTN_W + 2 * KV_W + 2 * LRU_W + S5_W
D_FF = 2816
NEG = -1e30
EPS = 1e-6
LOG2E = 1.4426950408889634

LANES = 128
SUBLANES = 8
TOKEN_TILE = 256
TILES_PER_STEP = 2
SCAN_CHUNK = 128
SCAN_PITCH = SCAN_CHUNK + SUBLANES // 2
FF_CHUNK = 256
VMEM_LIMIT = 56 * 1024 * 1024

_F32 = jnp.float32
_BF16 = jnp.bfloat16


def _cparams(n_axes):
    return pltpu.CompilerParams(dimension_semantics=("arbitrary",) * n_axes,
                                vmem_limit_bytes=VMEM_LIMIT)


def _const_spec(shape):
    nd = len(shape)
    return pl.BlockSpec(shape, lambda *_: (0,) * nd)


def _dot(a, b):
    return jnp.dot(a, b, preferred_element_type=_F32)


def _lane_iota(shape):
    return lax.broadcasted_iota(jnp.int32, shape, len(shape) - 1)


def _mod_kernel(c_ref, w_ref, b_ref, o_ref):
    a = jax.nn.silu(c_ref[...])
    o_ref[...] = jnp.dot(a, w_ref[...], preferred_element_type=_F32,
                         precision=lax.Precision.HIGHEST) + b_ref[...]


def _modulation(cvecs, w_mod, b_mod):
    depth, d, n = w_mod.shape
    nt = 1536
    return pl.pallas_call(
        _mod_kernel,
        grid=(depth, n // nt),
        in_specs=[pl.BlockSpec((SUBLANES, d), lambda l, j: (0, 0)),
                  pl.BlockSpec((None, d, nt), lambda l, j: (l, 0, j)),
                  pl.BlockSpec((None, 1, nt), lambda l, j: (l, 0, j))],
        out_specs=pl.BlockSpec((None, SUBLANES, nt), lambda l, j: (l, 0, j)),
        out_shape=jax.ShapeDtypeStruct((depth, SUBLANES, n), _F32),
        compiler_params=_cparams(2),
        name="modulation",
    )(cvecs, w_mod, b_mod.reshape(depth, 1, n))


def _rms_rows(x):
    return x * lax.rsqrt(jnp.mean(x * x, axis=-1, keepdims=True) + EPS)


def _head_norm_rope(t, gain, cos, sin):
    lane = _lane_iota(t.shape)
    low = lane < HEAD_DIM
    sq = t * t
    s_low = jnp.sum(jnp.where(low, sq, 0.0), axis=-1, keepdims=True)
    s_all = jnp.sum(sq, axis=-1, keepdims=True)
    ms = jnp.where(low, s_low, s_all - s_low) * (1.0 / HEAD_DIM)
    tn = t * lax.rsqrt(ms + EPS) * gain
    partner = jnp.where((lane % 32) < 16, pltpu.roll(tn, LANES - 16, 1), pltpu.roll(tn, 16, 1))
    return tn * cos + partner * sin


def _inproj_tile(is_ctx, rows, x_ref, c_ref, mod_ref, cos_ref, sin_ref, g_ref, w_ref, qg_ref, kg_ref,
                 q_ref, kv_ref, gate_ref, xl_ref, u_ref):
    x = jnp.where(is_ctx, c_ref[...], x_ref[...])
    scale = g_ref[...] * (1.0 + mod_ref[1:2, :])
    h = _rms_rows(x) * scale + mod_ref[0:1, :]
    proj = _dot(h.astype(_BF16), w_ref[...])
    cos = cos_ref[...]
    sin = sin_ref[...]
    for g in range(GQA_GROUP):
        blk = proj[:, g * LANES:(g + 1) * LANES]
        q_ref[rows, g * LANES:(g + 1) * LANES] = _head_norm_rope(blk, qg_ref[...], cos, sin).astype(_BF16)
    o = ATTN_W
    kv_ref[rows, :KV_W] = _head_norm_rope(proj[:, o:o + KV_W], kg_ref[...], cos, sin).astype(_BF16)
    o += KV_W
    kv_ref[rows, KV_W:] = proj[:, o:o + KV_W].astype(_BF16)
    o += KV_W
    gate_ref[rows, :] = proj[:, o:o + LRU_W].astype(_BF16)
    o += LRU_W
    xl_ref[rows, :] = proj[:, o:o + LRU_W]
    o += LRU_W
    u_ref[rows, :] = proj[:, o:o + S5_W]


def _inproj_kernel(*refs, n_lat_tiles, n_tiles):
    n_in = 5
    shared_and_outs = refs[TILES_PER_STEP * n_in:]
    tm = refs[0].shape[0]
    for k in range(TILES_PER_STEP):
        tile = pl.program_id(0) * TILES_PER_STEP + k
        is_ctx = (tile % n_tiles) >= n_lat_tiles
        _inproj_tile(is_ctx, slice(k * tm, (k + 1) * tm), *refs[k * n_in:(k + 1) * n_in], *shared_and_outs)


def _flat_tile(s, k, n_tiles):
    t = s * TILES_PER_STEP + k
    return t // n_tiles, t % n_tiles


def _source_specs(bt, tm, d, n_lat_tiles, ctx_tile0):
    return [
        pl.BlockSpec((None, tm, d), lambda s: (bt(s)[0], jnp.minimum(bt(s)[1], n_lat_tiles - 1), 0)),
        pl.BlockSpec((None, tm, d), lambda s: (bt(s)[0], ctx_tile0 + jnp.maximum(bt(s)[1] - n_lat_tiles, 0), 0)),
        pl.BlockSpec((None, None, 6, d), lambda s: (bt(s)[0], (bt(s)[1] >= n_lat_tiles).astype(jnp.int32), 0, 0))]


def _inproj(x_src, c_src, ctx_tile0, msel, norm_g, w_in_p, qg, kg, cos_t, sin_t, n_lat_tiles, n_tiles):
    b, _, d = x_src.shape
    tm = TOKEN_TILE
    n_flat = b * n_tiles
    assert n_flat % TILES_PER_STEP == 0
    widths = [(ATTN_W, _BF16), (2 * KV_W, _BF16), (LRU_W, _BF16), (LRU_W, _F32), (S5_W, _F32)]
    in_specs, in_args = [], []
    for k in range(TILES_PER_STEP):
        bt = functools.partial(_flat_tile, k=k, n_tiles=n_tiles)
        in_specs += [*_source_specs(bt, tm, d, n_lat_tiles, ctx_tile0),
                     pl.BlockSpec((tm, LANES), lambda s, bt=bt: (bt(s)[1], 0)),
                     pl.BlockSpec((tm, LANES), lambda s, bt=bt: (bt(s)[1], 0))]
        in_args += [x_src, c_src, msel, cos_t, sin_t]
    in_specs += [_const_spec((1, d)), _const_spec((d, IN_W)), _const_spec((1, LANES)), _const_spec((1, LANES))]
    res = pl.pallas_call(
        functools.partial(_inproj_kernel, n_lat_tiles=n_lat_tiles, n_tiles=n_tiles),
        grid=(n_flat // TILES_PER_STEP,),
        in_specs=in_specs,
        out_specs=[pl.BlockSpec((TILES_PER_STEP * tm, w), lambda s: (s, 0)) for w, _ in widths],
        out_shape=[jax.ShapeDtypeStruct((n_flat * tm, w), dt) for w, dt in widths],
        compiler_params=_cparams(1),
        name="inproj",
    )(*in_args, norm_g, w_in_p, qg, kg)
    return [r.reshape(b, n_tiles * tm, r.shape[-1]) for r in res]


def _attend(q_blk, pieces, sink_ref):
    n_slab = N_KV_HEADS * GQA_GROUP
    lane = _lane_iota((BLOCK, LANES))
    low = lane < HEAD_DIM
    zero = jnp.zeros((BLOCK, LANES), _BF16)
    slabs = [[None] * GQA_GROUP for _ in range(N_KV_HEADS)]
    for g in range(GQA_GROUP):
        blk = q_blk[:, g * LANES:(g + 1) * LANES]
        slabs[0][g] = jnp.where(low, blk, zero)
        slabs[1][g] = jnp.where(low, zero, blk)
    qst = jnp.concatenate(slabs[0] + slabs[1], axis=0)
    half = GQA_GROUP * BLOCK
    m = jnp.concatenate([jnp.full((BLOCK, LANES), sink_ref[s] * LOG2E, _F32) for s in range(n_slab)], axis=0)
    lane_h = _lane_iota((half, LANES)) < HEAD_DIM
    acc = [jnp.where(lane_h, 0.0, 1.0), jnp.where(lane_h, 1.0, 0.0)]
    for t0 in range(0, len(pieces), 2):
        tile = pieces[t0:t0 + 2]
        keys = jnp.concatenate([p[0] for p in tile], axis=0)
        vals = jnp.concatenate([p[1] for p in tile], axis=0)
        s = lax.dot_general(qst, keys, (((1,), (1,)), ((), ())), preferred_element_type=_F32)
        if any(p[2] is not None for p in tile):
            bias = jnp.concatenate([p[2] if p[2] is not None else jnp.zeros((BLOCK, BLOCK), _F32) for p in tile],
                                   axis=1)
            s = jnp.concatenate([s[i * BLOCK:(i + 1) * BLOCK] + bias for i in range(n_slab)], axis=0)
        m_new = jnp.maximum(m, jnp.broadcast_to(jnp.max(s, axis=-1, keepdims=True), m.shape))
        alpha = jnp.exp2(m - m_new)
        p = jnp.exp2(s - jnp.concatenate([m_new] * (s.shape[1] // LANES), axis=1)).astype(_BF16)
        m = m_new
        low_v = _lane_iota(vals.shape) < HEAD_DIM
        one = jnp.ones(vals.shape, _BF16)
        vext = (jnp.where(low_v, vals, one), jnp.where(low_v, one, vals))
        for hk in range(N_KV_HEADS):
            rs = slice(hk * half, (hk + 1) * half)
            acc[hk] = alpha[rs] * acc[hk] + _dot(p[rs], vext[hk])
    outs = []
    for g in range(GQA_GROUP):
        a0 = acc[0][g * BLOCK:(g + 1) * BLOCK]
        a1 = acc[1][g * BLOCK:(g + 1) * BLOCK]
        num = jnp.where(low, a0, a1)
        den = pltpu.roll(jnp.where(low, a1, a0), HEAD_DIM, 1)
        outs.append((num / den).astype(_BF16))
    return outs


def _attn_kernel(sink_ref, q_ref, kvp_ref, kvc_ref, kvn_ref, kvx_ref, o_ref, *, n_lat_pairs):
    qi = pl.program_id(1)
    is_lat = qi < n_lat_pairs
    row = lax.broadcasted_iota(jnp.int32, (BLOCK, BLOCK), 0)
    col = lax.broadcasted_iota(jnp.int32, (BLOCK, BLOCK), 1)
    open_ = jnp.where(is_lat, 0.0, NEG)
    tri_prev = jnp.where(col >= row, open_, NEG)
    tri_next = jnp.where(col <= row, open_, NEG)
    full = jnp.full((BLOCK, BLOCK), open_, _F32)
    shut = jnp.full((BLOCK, BLOCK), NEG, _F32)
    has_prev = is_lat & (qi > 0)
    has_next = qi < n_lat_pairs - 1

    def kv(ref, r0=0):
        return ref[r0:r0 + BLOCK, :KV_W], ref[r0:r0 + BLOCK, KV_W:]

    ctx = [kv(kvx_ref, c * BLOCK) + (None,) for c in range(kvx_ref.shape[0] // BLOCK)]
    lo, hi = kv(kvc_ref, 0), kv(kvc_ref, BLOCK)
    first = [kv(kvp_ref) + (jnp.where(has_prev, tri_prev, shut),), lo + (full,), hi + (tri_next,)] + ctx
    second = [lo + (tri_prev,), hi + (full,), kv(kvn_ref) + (jnp.where(has_next, tri_next, shut),)] + ctx
    for half, pieces in enumerate((first, second)):
        outs = _attend(q_ref[half * BLOCK:(half + 1) * BLOCK, :], pieces, sink_ref)
        for g in range(GQA_GROUP):
            o_ref[half * BLOCK:(half + 1) * BLOCK, g * LANES:(g + 1) * LANES] = outs[g]


def _attention(q, kv, sink, n_lat, n_ctx, with_ctx_queries):
    b, s, _ = q.shape
    pair = 2 * BLOCK
    nlp = n_lat // pair
    nq = s // pair if with_ctx_queries else nlp
    last = s // BLOCK - 1
    halo = lambda f: pl.BlockSpec((None, BLOCK, 2 * KV_W), lambda bi, i: (bi, f(i), 0))
    return pl.pallas_call(
        functools.partial(_attn_kernel, n_lat_pairs=nlp),
        grid=(b, nq),
        in_specs=[pl.BlockSpec(memory_space=pltpu.SMEM),
                  pl.BlockSpec((None, pair, ATTN_W), lambda bi, i: (bi, i, 0)),
                  halo(lambda i: jnp.maximum(2 * i - 1, 0)),
                  pl.BlockSpec((None, pair, 2 * KV_W), lambda bi, i: (bi, i, 0)),
                  halo(lambda i: jnp.minimum(2 * i + 2, last)),
                  pl.BlockSpec((None, n_ctx, 2 * KV_W), lambda bi, i: (bi, n_lat // n_ctx, 0))],
        out_specs=pl.BlockSpec((None, pair, ATTN_W), lambda bi, i: (bi, i, 0)),
        out_shape=jax.ShapeDtypeStruct((b, nq * pair, ATTN_W), _BF16),
        compiler_params=_cparams(2),
        name="attention",
    )(sink, q, kv, kv, kv, kv)


def _shift_rows(x, n_rows_out, shift):
    pad = jnp.zeros((n_rows_out - x.shape[0], x.shape[1]), x.dtype)
    return pltpu.roll(jnp.concatenate([x, pad], axis=0), shift, 0)


def _scan_kernel(xf_ref, xfp_ref, xfn_ref, uf_ref, xb_ref, xbp_ref, xbn_ref, ub_ref,
                 cw_ref, cb_ref, wa_ref, wx_ref, ba_ref, bx_ref, nsp_ref,
                 bd_ref, cm_ref, ar_ref, ai_ref,
                 of_ref, ob_ref,
                 s5_in, s5_out, lru_in, lru_out, s5_state, lru_state, *, n_lat_chunks, n_ctx_chunks):
    i = pl.program_id(0)
    n_b = xf_ref.shape[0]
    t_len = SCAN_CHUNK
    pitch = SCAN_PITCH
    off = pitch - t_len
    t_win = t_len + SUBLANES
    n_slab = 2 * S5_FLAT // (2 * LANES)
    half_w = n_slab * LANES
    n_tot = n_lat_chunks + n_ctx_chunks
    ic = jnp.minimum(i, n_tot - 1)
    chunk_f = jnp.where(ic < n_ctx_chunks, n_lat_chunks + ic, ic - n_ctx_chunks)
    chunk_b = n_tot - 1 - ic

    @pl.when(i == 0)
    def _():
        s5_state[...] = jnp.zeros_like(s5_state)
        lru_state[...] = jnp.zeros_like(lru_state)
        s5_out[...] = jnp.zeros_like(s5_out)
        lru_out[...] = jnp.zeros_like(lru_out)

    def seg_first(c):
        return (c == 0) | (c == n_lat_chunks)

    def seg_last(c):
        return (c == n_lat_chunks - 1) | (c == n_tot - 1)

    def even_rows(b):
        return pl.ds(2 * b * pitch, t_len)

    def odd_rows(b):
        return pl.ds((2 * b + 1) * pitch - off, t_win)

    for d, o_ref in ((0, of_ref), (1, ob_ref)):
        h_even = jnp.concatenate(
            [jnp.concatenate([s5_out[d, sl, even_rows(b), :] for sl in range(n_slab)], axis=1)
             for b in range(n_b)], axis=0)
        h_odd = jnp.concatenate(
            [jnp.concatenate([s5_out[d, sl, odd_rows(b), :] for sl in range(n_slab)], axis=1)
             for b in range(n_b)], axis=0)
        y_even = _dot(h_even.astype(_BF16), cm_ref[d, :half_w, :])
        y_odd = _dot(h_odd.astype(_BF16), cm_ref[d, half_w:, :])
        for b in range(n_b):
            y = (y_even[b * t_len:(b + 1) * t_len]
                 + pltpu.roll(y_odd[b * t_win:(b + 1) * t_win], t_win - off, 0)[:t_len])
            o_ref[b] = jnp.concatenate(
                [lru_out[d, even_rows(b), :],
                 pltpu.roll(lru_out[d, odd_rows(b), :], t_win - off, 0)[:t_len], y], axis=1).astype(_BF16)

    dirs = ((0, xf_ref, xfp_ref, xfn_ref, uf_ref, chunk_f), (1, xb_ref, xbp_ref, xbn_ref, ub_ref, chunk_b))
    for d, x_ref, xp_ref, xn_ref, u_ref, chunk in dirs:
        keep_prev = jnp.where(seg_first(chunk), 0.0, 1.0)
        keep_next = jnp.where(seg_last(chunk), 0.0, 1.0)
        n_e = t_len + 2 * SUBLANES
        xcs = []
        for b in range(n_b):
            xe = jnp.concatenate([x_ref[b], xn_ref[b] * keep_next, xp_ref[b] * keep_prev], axis=0)
            acc = cb_ref[...] + cw_ref[1:2, :] * x_ref[b]
            for j, shift in ((0, 1), (2, n_e - 1), (3, n_e - 2)):
                acc = acc + cw_ref[j:j + 1, :] * pltpu.roll(xe, shift, 0)[:t_len]
            xcs.append(acc)
        xc = jnp.concatenate(xcs, axis=0)
        xcb = xc.astype(_BF16)
        r = jax.nn.sigmoid(_dot(xcb, wa_ref[d]) + ba_ref[d])
        gi = jax.nn.sigmoid(_dot(xcb, wx_ref[d]) + bx_ref[d])
        log_a = nsp_ref[d] * r
        a = jnp.exp(log_a)
        bco = jnp.sqrt(-jnp.tanh(log_a) * (a * a + 1.0)) * gi * xc
        for b in range(n_b):
            rb = slice(b * t_len, (b + 1) * t_len)
            lru_in[d, 0, even_rows(b), :] = a[rb, :LANES]
            lru_in[d, 1, even_rows(b), :] = bco[rb, :LANES]
            lru_in[d, 0, odd_rows(b), :] = _shift_rows(a[rb, LANES:], t_win, off)
            lru_in[d, 1, odd_rows(b), :] = _shift_rows(bco[rb, LANES:], t_win, off)
        u_even = jnp.concatenate([u_ref[b] for b in range(n_b)], axis=0)
        u_odd = jnp.concatenate([_shift_rows(u_ref[b], t_win, off) for b in range(n_b)], axis=0)
        drive_even = _dot(u_even.astype(_BF16), bd_ref[d, :, :half_w])
        drive_odd = _dot(u_odd.astype(_BF16), bd_ref[d, :, half_w:])
        for b in range(n_b):
            for sl in range(n_slab):
                cs = slice(sl * LANES, (sl + 1) * LANES)
                s5_in[d, sl, even_rows(b), :] = drive_even[b * t_len:(b + 1) * t_len, cs]
                s5_in[d, sl, odd_rows(b), :] = drive_odd[b * t_win:(b + 1) * t_win, cs]

    n_k = n_slab // 2
    coef = [[(ar_ref[d, :, k * LANES:(k + 1) * LANES], ai_ref[d, :, k * LANES:(k + 1) * LANES])
             for k in range(n_k)] for d in range(N_DIR)]

    def step(it, carry):
        new = []
        for d in range(N_DIR):
            t = it if d == 0 else t_len - 1 - it
            rows = pl.ds(t, SUBLANES, stride=pitch)
            hs, hl = carry[d]
            nhs = [None] * n_slab
            for k in range(n_k):
                ar, ai = coef[d][k]
                hr, hi = hs[k], hs[n_k + k]
                nr = ar * hr - ai * hi + s5_in[d, k, rows, :]
                ni = ar * hi + ai * hr + s5_in[d, n_k + k, rows, :]
                s5_out[d, k, rows, :] = nr
                s5_out[d, n_k + k, rows, :] = ni
                nhs[k], nhs[n_k + k] = nr, ni
            nhl = lru_in[d, 0, rows, :] * hl + lru_in[d, 1, rows, :]
            lru_out[d, rows, :] = nhl
            new.append((tuple(nhs), nhl))
        return tuple(new)

    @pl.when(i < n_tot)
    def _():
        init = tuple((tuple(s5_state[d, sl] for sl in range(n_slab)), lru_state[d]) for d in range(N_DIR))
        fin = lax.fori_loop(0, t_len, step, init, unroll=4)
        for d in range(N_DIR):
            for sl in range(n_slab):
                s5_state[d, sl] = fin[d][0][sl]
            lru_state[d] = fin[d][1]


def _scans(xl, u, sp, n_lat, n_ctx):
    b, s, _ = xl.shape
    t = SCAN_CHUNK
    nl, nc = n_lat // t, n_ctx // t
    n_tot = nl + nc
    n8 = s // SUBLANES
    per8 = t // SUBLANES
    cf = lambda i: jnp.where(i < nc, nl + i, i - nc)
    cb = lambda i: n_tot - 1 - i
    cur = lambda i: jnp.minimum(i, n_tot - 1)
    done = lambda i: jnp.maximum(i - 1, 0)
    main = lambda f: pl.BlockSpec((b, t, LRU_W), lambda i: (0, f(cur(i)), 0))
    outb = lambda f: pl.BlockSpec((b, t, LRU_W + S5_W), lambda i: (0, f(done(i)), 0))
    prev = lambda f: pl.BlockSpec((b, SUBLANES, LRU_W), lambda i: (0, jnp.maximum(f(cur(i)) * per8 - 1, 0), 0))
    nxt = lambda f: pl.BlockSpec((b, SUBLANES, LRU_W),
                                 lambda i: (0, jnp.minimum((f(cur(i)) + 1) * per8, n8 - 1), 0))
    n_slab = 2 * S5_FLAT // (2 * LANES)
    n_rows = SUBLANES * SCAN_PITCH
    weights = [sp["conv_w"], sp["conv_b"], sp["wa"], sp["wx"], sp["ba"], sp["bx"], sp["nsp"],
               sp["bd"], sp["cm"], sp["ar"], sp["ai"]]
    out_sd = jax.ShapeDtypeStruct((b, s, LRU_W + S5_W), _BF16)
    return pl.pallas_call(
        functools.partial(_scan_kernel, n_lat_chunks=nl, n_ctx_chunks=nc),
        grid=(n_tot + 1,),
        in_specs=[main(cf), prev(cf), nxt(cf), main(cf), main(cb), prev(cb), nxt(cb), main(cb)]
                 + [_const_spec(w.shape) for w in weights],
        out_specs=[outb(cf), outb(cb)],
        out_shape=[out_sd] * 2,
        scratch_shapes=[pltpu.VMEM((N_DIR, n_slab, n_rows, LANES), _F32),
                        pltpu.VMEM((N_DIR, n_slab, n_rows, LANES), _F32),
                        pltpu.VMEM((N_DIR, 2, n_rows, LANES), _F32),
                        pltpu.VMEM((N_DIR, n_rows, LANES), _F32),
                        pltpu.VMEM((N_DIR, n_slab, SUBLANES, LANES), _F32),
                        pltpu.VMEM((N_DIR, SUBLANES, LANES), _F32)],
        compiler_params=_cparams(1),
        name="scans",
    )(xl, xl, xl, u, xl, xl, xl, u, *weights)


def _merge_tile(is_ctx, rows, x_ref, c_ref, mod_ref, attn_ref, gate_ref, of_ref, ob_ref, u_ref,
                mg_ref, dskip_ref, wglu_ref, wout_ref, o_ref):
    x = jnp.where(is_ctx, c_ref[...], x_ref[...])
    both = of_ref[...].astype(_F32) + ob_ref[...].astype(_F32)
    o_lru = jax.nn.gelu(gate_ref[...].astype(_F32)) * both[:, :LRU_W]
    y = u_ref[...] * dskip_ref[...] + both[:, LRU_W:]
    z = jax.nn.gelu(y)
    o_s5 = z * jax.nn.sigmoid(_dot(z.astype(_BF16), wglu_ref[...]))
    mg = mg_ref[...]
    parts = [_rms_rows(attn_ref[...].astype(_F32)) * mg[:, :ATTN_W],
             _rms_rows(o_lru) * mg[:, ATTN_W:ATTN_W + LRU_W],
             _rms_rows(o_s5) * mg[:, ATTN_W + LRU_W:]]
    cat = jnp.concatenate([p.astype(_BF16) for p in parts], axis=1)
    o_ref[rows, :] = x + mod_ref[2:3, :] * _dot(cat, wout_ref[...])


def _merge_kernel(*refs, n_lat_tiles, n_tiles):
    n_in = 8
    shared_and_out = refs[TILES_PER_STEP * n_in:]
    tm = refs[0].shape[0]
    for k in range(TILES_PER_STEP):
        tile = pl.program_id(0) * TILES_PER_STEP + k
        is_ctx = (tile % n_tiles) >= n_lat_tiles
        _merge_tile(is_ctx, slice(k * tm, (k + 1) * tm), *refs[k * n_in:(k + 1) * n_in], *shared_and_out)


def _merge(x_src, c_src, ctx_tile0, msel, attn, gate, of, ob, u, mg_p, dskip, w_glu, w_out_p,
           n_lat_tiles, n_tiles):
    b, _, d = x_src.shape
    tm = TOKEN_TILE
    n_flat = b * n_tiles
    assert n_flat % TILES_PER_STEP == 0
    in_specs, in_args = [], []
    for k in range(TILES_PER_STEP):
        bt = functools.partial(_flat_tile, k=k, n_tiles=n_tiles)
        tok = lambda w, bt=bt: pl.BlockSpec((None, tm, w), lambda s, bt=bt: (*bt(s), 0))
        in_specs += [*_source_specs(bt, tm, d, n_lat_tiles, ctx_tile0),
                     tok(ATTN_W), tok(LRU_W), tok(LRU_W + S5_W), tok(LRU_W + S5_W), tok(S5_W)]
        in_args += [x_src, c_src, msel, attn, gate, of, ob, u]
    in_specs += [_const_spec((1, d)), _const_spec((1, S5_W)), _const_spec((S5_W, S5_W)), _const_spec((d, d))]
    out = pl.pallas_call(
        functools.partial(_merge_kernel, n_lat_tiles=n_lat_tiles, n_tiles=n_tiles),
        grid=(n_flat // TILES_PER_STEP,),
        in_specs=in_specs,
        out_specs=pl.BlockSpec((TILES_PER_STEP * tm, d), lambda s: (s, 0)),
        out_shape=jax.ShapeDtypeStruct((n_flat * tm, d), _F32),
        compiler_params=_cparams(1),
        name="merge",
    )(*in_args, mg_p, dskip, w_glu, w_out_p)
    return out.reshape(b, n_tiles * tm, d)


def _ffn_tile(i, rows, x_ref, xp_ref, xn_ref, mod_ref, g_ref, wup_ref, cw_ref, cb_ref, wd_ref, o_ref, act_ref,
              n_lat_tiles, n_tiles):
    tm = x_ref.shape[0]
    n_e = tm + 2 * SUBLANES
    first = (i == 0) | (i == n_lat_tiles)
    last = (i == n_lat_tiles - 1) | (i == n_tiles - 1)
    x = x_ref[...]
    xe = jnp.concatenate([x, xn_ref[...], xp_ref[...]], axis=0)
    scale = g_ref[...] * (1.0 + mod_ref[4:5, :])
    rowe = lax.broadcasted_iota(jnp.int32, (n_e, 1), 0)
    pad_row = ((rowe >= jnp.where(first, tm + SUBLANES, n_e))
               | ((rowe >= tm) & (rowe < jnp.where(last, tm + SUBLANES, tm))))
    he = jnp.where(pad_row, 0.0, _rms_rows(xe) * scale + mod_ref[3:4, :]).astype(_BF16)
    hc = he[:tm]
    for f in range(D_FF // FF_CHUNK):
        cs = slice(f * FF_CHUNK, (f + 1) * FF_CHUNK)
        ge = _dot(he, wup_ref[:, cs])
        gv = _dot(hc, wup_ref[:, D_FF + f * FF_CHUNK:D_FF + (f + 1) * FF_CHUNK])
        conv = (cb_ref[:, cs] + cw_ref[1:2, cs] * ge[:tm]
                + cw_ref[0:1, cs] * pltpu.roll(ge, 1, 0)[:tm]
                + cw_ref[2:3, cs] * pltpu.roll(ge, n_e - 1, 0)[:tm])
        act_ref[:, cs] = (jax.nn.gelu(conv) * gv).astype(_BF16)
    o_ref[rows, :] = x + mod_ref[5:6, :] * _dot(act_ref[...], wd_ref[...])


def _ffn_kernel(*refs, n_lat_tiles, n_tiles):
    n_in = 4
    g_ref, wup_ref, cw_ref, cb_ref, wd_ref, o_ref, act_ref = refs[TILES_PER_STEP * n_in:]
    tm = refs[0].shape[0]
    for k in range(TILES_PER_STEP):
        tile = pl.program_id(0) * TILES_PER_STEP + k
        _ffn_tile(tile % n_tiles, slice(k * tm, (k + 1) * tm), *refs[k * n_in:(k + 1) * n_in],
                  g_ref, wup_ref, cw_ref, cb_ref, wd_ref, o_ref, act_ref.at[k], n_lat_tiles, n_tiles)


def _ffn(x1, msel, norm_g, layer, w_up, cw, cb, wd, n_lat_tiles, n_tiles):
    b, s, d = x1.shape
    tm = TOKEN_TILE
    per8 = tm // SUBLANES
    n_flat = b * n_tiles
    assert s == n_tiles * tm and n_flat % TILES_PER_STEP == 0
    x1f = x1.reshape(b * s, d)
    n8 = b * s // SUBLANES
    resident = lambda a: pl.BlockSpec((None,) + a.shape[1:], lambda *_: (layer,) + (0,) * (a.ndim - 1),
                                      pipeline_mode=pl.Buffered(1))
    in_specs, in_args = [], []
    for k in range(TILES_PER_STEP):
        ft = lambda st, k=k: st * TILES_PER_STEP + k
        in_specs += [
            pl.BlockSpec((tm, d), lambda st, ft=ft: (ft(st), 0)),
            pl.BlockSpec((SUBLANES, d), lambda st, ft=ft: (jnp.maximum(ft(st) * per8 - 1, 0), 0)),
            pl.BlockSpec((SUBLANES, d), lambda st, ft=ft: (jnp.minimum((ft(st) + 1) * per8, n8 - 1), 0)),
            pl.BlockSpec((None, None, 6, d),
                         lambda st, ft=ft: (ft(st) // n_tiles, (ft(st) % n_tiles >= n_lat_tiles).astype(jnp.int32),
                                            0, 0))]
        in_args += [x1f, x1f, x1f, msel]
    in_specs += [_const_spec((1, d)), resident(w_up), _const_spec(cw.shape), _const_spec(cb.shape), resident(wd)]
    out = pl.pallas_call(
        functools.partial(_ffn_kernel, n_lat_tiles=n_lat_tiles, n_tiles=n_tiles),
        grid=(n_flat // TILES_PER_STEP,),
        in_specs=in_specs,
        out_specs=pl.BlockSpec((TILES_PER_STEP * tm, d), lambda st: (st, 0)),
        out_shape=jax.ShapeDtypeStruct((b * s, d), _F32),
        scratch_shapes=[pltpu.VMEM((TILES_PER_STEP, tm, D_FF), _BF16)],
        compiler_params=_cparams(1),
        name="ffn",
    )(*in_args, norm_g, w_up, cw, cb, wd)
    return out.reshape(b, s, d)


def _rope_tables(n_lat, n_ctx):
    t = jnp.arange(n_lat)
    row = (t // GRID_W).astype(_F32)
    col = (t % GRID_W).astype(_F32)
    n_freq = HEAD_DIM // 4
    inv = ROPE_BASE ** (-jnp.arange(n_freq, dtype=_F32) / n_freq)
    ang_r = row[:, None] * inv
    ang_c = col[:, None] * inv
    cos_h = jnp.concatenate([jnp.cos(ang_r), jnp.cos(ang_r), jnp.cos(ang_c), jnp.cos(ang_c)], axis=1)
    sin_h = jnp.concatenate([-jnp.sin(ang_r), jnp.sin(ang_r), -jnp.sin(ang_c), jnp.sin(ang_c)], axis=1)
    cos_t = jnp.concatenate([jnp.tile(cos_h, (1, 2)), jnp.ones((n_ctx, LANES), _F32)], axis=0)
    sin_t = jnp.concatenate([jnp.tile(sin_h, (1, 2)), jnp.zeros((n_ctx, LANES), _F32)], axis=0)
    return cos_t, sin_t


def _heads_kv_minor(a, axis):
    shape = a.shape
    a = a.reshape(shape[:axis] + (N_KV_HEADS, GQA_GROUP, HEAD_DIM) + shape[axis + 1:])
    return jnp.swapaxes(a, axis, axis + 1).reshape(shape)


def _block_diag(blocks):
    n, r, c = blocks.shape
    tiled = jnp.tile(blocks.reshape(n * r, c), (1, n))
    row = lax.broadcasted_iota(jnp.int32, (n * r, n * c), 0) // r
    col = lax.broadcasted_iota(jnp.int32, (n * r, n * c), 1) // c
    return jnp.where(row == col, tiled, 0.0)


def _s5_params(lam_re, lam_im, log_step, b_re, b_im, c_re, c_im):
    lr = jnp.minimum(lam_re.astype(_F32), -1e-4)
    li = lam_im.astype(_F32)
    dt = jnp.exp(log_step.astype(_F32))[:, None]
    mag = jnp.exp(lr * dt)
    ab_re = mag * jnp.cos(li * dt)
    ab_im = mag * jnp.sin(li * dt)
    nr = ab_re - 1
    den = lr * lr + li * li
    cr = ((nr * lr + ab_im * li) / den)[..., None]
    ci = ((ab_im * lr - nr * li) / den)[..., None]
    br = b_re.astype(_F32)
    bi = b_im.astype(_F32)
    bb_re = cr * br - ci * bi
    bb_im = cr * bi + ci * br
    half = S5_FLAT // 2
    d_re = _block_diag(jnp.swapaxes(bb_re, 1, 2))
    d_im = _block_diag(jnp.swapaxes(bb_im, 1, 2))
    bd = jnp.concatenate([d_re[:, :half], d_im[:, :half], d_re[:, half:], d_im[:, half:]], axis=1)
    r_re = _block_diag(jnp.swapaxes(c_re.astype(_F32), 1, 2))
    r_im = -_block_diag(jnp.swapaxes(c_im.astype(_F32), 1, 2))
    cm = jnp.concatenate([r_re[:half], r_im[:half], r_re[half:], r_im[half:]], axis=0)
    a_re = jnp.tile(ab_re.reshape(2, half), (SUBLANES // 2, 1))
    a_im = jnp.tile(ab_im.reshape(2, half), (SUBLANES // 2, 1))
    return bd.astype(_BF16), cm.astype(_BF16), a_re, a_im


def _per_layer_dir(fn):
    return jax.vmap(jax.vmap(fn))


def kernel(x, c, ctx, c_ctx, w_mod, b_mod, norm1_g, norm2_g, w_in, q_norm_g, k_norm_g, attn_sink,
           lru_conv_w, lru_conv_b, lru_wa, lru_ba, lru_wx, lru_bx, lru_lambda,
           s5_lam_re, s5_lam_im, s5_log_step, s5_b_re, s5_b_im, s5_c_re, s5_c_im, s5_d, s5_w_glu,
           mix_g, w_out, ffn_w_up, ffn_conv_w, ffn_conv_b, ffn_w_down):
    n_b, n_lat, d = x.shape
    n_ctx = ctx.shape[1]
    depth = w_mod.shape[0]
    assert d == D_MODEL and n_b * 2 == SUBLANES
    assert n_lat % TOKEN_TILE == 0 and n_ctx % TOKEN_TILE == 0 and n_lat % n_ctx == 0
    n_lat_tiles = n_lat // TOKEN_TILE
    n_all_tiles = (n_lat + n_ctx) // TOKEN_TILE

    cvecs = jnp.concatenate([c, c_ctx[None], jnp.zeros((SUBLANES - n_b - 1, d), _F32)], axis=0)
    mod = _modulation(cvecs, w_mod, b_mod).reshape(depth, SUBLANES, 6, d)
    msel = jnp.stack([mod[:, :n_b], jnp.broadcast_to(mod[:, n_b:n_b + 1], (depth, n_b, 6, d))], axis=2)
    cos_t, sin_t = _rope_tables(n_lat, n_ctx)

    qg = jnp.tile(q_norm_g, (1, 2))[:, None] * (HEAD_DIM ** -0.5 * LOG2E)
    kg = jnp.tile(k_norm_g, (1, 2))[:, None]
    nsp = -LRU_C * jax.nn.softplus(-lru_lambda.astype(_F32))
    bd, cm, a_re, a_im = _per_layer_dir(_s5_params)(s5_lam_re, s5_lam_im, s5_log_step, s5_b_re, s5_b_im,
                                                    s5_c_re, s5_c_im)
    wa = _per_layer_dir(_block_diag)(lru_wa).astype(_BF16)
    wx = _per_layer_dir(_block_diag)(lru_wx).astype(_BF16)
    w_glu = s5_w_glu.astype(_BF16)
    w_up = ffn_w_up.astype(_BF16)
    w_down = ffn_w_down.astype(_BF16)

    x_src, c_src, ctx_tile0 = x, ctx, 0
    for l in range(depth):
        last = l == depth - 1
        w_in_p = jnp.concatenate([_heads_kv_minor(w_in[l][:, :ATTN_W], 1), w_in[l][:, ATTN_W:]], axis=1).astype(_BF16)
        w_out_p = jnp.concatenate([_heads_kv_minor(w_out[l][:ATTN_W], 0), w_out[l][ATTN_W:]], axis=0).astype(_BF16)
        mix_g_p = jnp.concatenate([_heads_kv_minor(mix_g[l][:ATTN_W], 0), mix_g[l][ATTN_W:]])[None]
        sp = {
            "conv_w": lru_conv_w[l], "conv_b": lru_conv_b[l][None], "wa": wa[l], "wx": wx[l],
            "ba": lru_ba[l][:, None, :], "bx": lru_bx[l][:, None, :], "nsp": nsp[l][:, None, :],
            "bd": bd[l], "cm": cm[l], "ar": a_re[l], "ai": a_im[l],
        }

        q, kv, gate, xl, u = _inproj(x_src, c_src, ctx_tile0, msel[l], norm1_g[l][None], w_in_p, qg[l], kg[l],
                                     cos_t, sin_t, n_lat_tiles, n_all_tiles)
        attn = _attention(q, kv, attn_sink[l], n_lat, n_ctx, with_ctx_queries=not last)
        of, ob = _scans(xl, u, sp, n_lat, n_ctx)
        n_tiles = n_lat_tiles if last else n_all_tiles
        x1 = _merge(x_src, c_src, ctx_tile0, msel[l], attn, gate, of, ob, u, mix_g_p, s5_d[l][None],
                    w_glu[l], w_out_p, n_lat_tiles, n_tiles)
        x_all = _ffn(x1, msel[l], norm2_g[l][None], l, w_up, ffn_conv_w[l], ffn_conv_b[l][None], w_down,
                     n_lat_tiles, n_tiles)
        x_src, c_src, ctx_tile0 = x_all, x_all, n_lat_tiles
    return x_all
```

```python
import functools

import jax
import jax.numpy as jnp
from jax import lax
from jax.experimental import pallas as pl
from jax.experimental.pallas import tpu as pltpu

D_MODEL = 1024
HEAD_DIM = 64
N_Q_HEADS = 8
N_KV_HEADS = 2
GQA_GROUP = N_Q_HEADS // N_KV_HEADS
ATTN_W = N_Q_HEADS * HEAD_DIM
KV_W = N_KV_HEADS * HEAD_DIM
WINDOW = 128
BLOCK = 128
GRID_W = 64
ROPE_BASE = 10000.0
LRU_W = 256
LRU_HEADS = 4
LRU_HEAD_W = LRU_W // LRU_HEADS
LRU_C = 8.0
S5_W = 256
S5_GROUP = 16
S5_GROUPS = S5_W // S5_GROUP
S5_STATE = 64
S5_FLAT = S5_GROUPS * S5_STATE
N_DIR = 2
IN_W = ATTN_W + 2 * KV_W + 2 * LRU_W + S5_W
D_FF = 2816
NEG = -1e30
EPS = 1e-6
LOG2E = 1.4426950408889634

LANES = 128
SUBLANES = 8
TOKEN_TILE = 256
TILES_PER_STEP = 4
FFN_TILES_PER_STEP = 2
SCAN_CHUNK = 128
SCAN_PITCH = SCAN_CHUNK + SUBLANES // 2
FF_CHUNK = 256
VMEM_LIMIT = 56 * 1024 * 1024

_F32 = jnp.float32
_BF16 = jnp.bfloat16


def _cparams(n_axes):
    return pltpu.CompilerParams(dimension_semantics=("arbitrary",) * n_axes,
                                vmem_limit_bytes=VMEM_LIMIT)


def _const_spec(shape):
    nd = len(shape)
    return pl.BlockSpec(shape, lambda *_: (0,) * nd)


def _dot(a, b):
    return jnp.dot(a, b, preferred_element_type=_F32)


def _lane_iota(shape):
    return lax.broadcasted_iota(jnp.int32, shape, len(shape) - 1)


def _mod_kernel(c_ref, w_ref, b_ref, o_ref):
    a = jax.nn.silu(c_ref[...])
    o_ref[...] = jnp.dot(a, w_ref[...], preferred_element_type=_F32,
                         precision=lax.Precision.HIGHEST) + b_ref[...]


def _modulation(cvecs, w_mod, b_mod):
    depth, d, n = w_mod.shape
    nt = 1536
    return pl.pallas_call(
        _mod_kernel,
        grid=(depth, n // nt),
        in_specs=[pl.BlockSpec((SUBLANES, d), lambda l, j: (0, 0)),
                  pl.BlockSpec((None, d, nt), lambda l, j: (l, 0, j)),
                  pl.BlockSpec((None, 1, nt), lambda l, j: (l, 0, j))],
        out_specs=pl.BlockSpec((None, SUBLANES, nt), lambda l, j: (l, 0, j)),
        out_shape=jax.ShapeDtypeStruct((depth, SUBLANES, n), _F32),
        compiler_params=_cparams(2),
        name="modulation",
    )(cvecs, w_mod, b_mod.reshape(depth, 1, n))


def _rms_rows(x):
    return x * lax.rsqrt(jnp.mean(x * x, axis=-1, keepdims=True) + EPS)


def _head_norm_rope(t, gain, cos, sin):
    lane = _lane_iota(t.shape)
    low = lane < HEAD_DIM
    sq = t * t
    s_low = jnp.sum(jnp.where(low, sq, 0.0), axis=-1, keepdims=True)
    s_all = jnp.sum(sq, axis=-1, keepdims=True)
    ms = jnp.where(low, s_low, s_all - s_low) * (1.0 / HEAD_DIM)
    tn = t * lax.rsqrt(ms + EPS) * gain
    partner = jnp.where((lane % 32) < 16, pltpu.roll(tn, LANES - 16, 1), pltpu.roll(tn, 16, 1))
    return tn * cos + partner * sin


def _inproj_tile(is_ctx, rows, x_ref, c_ref, mod_ref, cos_ref, sin_ref, g_ref, w_ref, qg_ref, kg_ref,
                 q_ref, kv_ref, gate_ref, xl_ref, u_ref):
    x = jnp.where(is_ctx, c_ref[...], x_ref[...])
    scale = g_ref[...] * (1.0 + mod_ref[1:2, :])
    h = _rms_rows(x) * scale + mod_ref[0:1, :]
    proj = _dot(h.astype(_BF16), w_ref[...])
    cos = cos_ref[...]
    sin = sin_ref[...]
    for g in range(GQA_GROUP):
        blk = proj[:, g * LANES:(g + 1) * LANES]
        q_ref[rows, g * LANES:(g + 1) * LANES] = _head_norm_rope(blk, qg_ref[...], cos, sin).astype(_BF16)
    o = ATTN_W
    kv_ref[rows, :KV_W] = _head_norm_rope(proj[:, o:o + KV_W], kg_ref[...], cos, sin).astype(_BF16)
    o += KV_W
    kv_ref[rows, KV_W:] = proj[:, o:o + KV_W].astype(_BF16)
    o += KV_W
    gate_ref[rows, :] = proj[:, o:o + LRU_W].astype(_BF16)
    o += LRU_W
    xl_ref[rows, :] = proj[:, o:o + LRU_W]
    o += LRU_W
    u_ref[rows, :] = proj[:, o:o + S5_W]


def _inproj_kernel(*refs, n_lat_tiles, n_tiles):
    n_in = 5
    shared_and_outs = refs[TILES_PER_STEP * n_in:]
    tm = refs[0].shape[0]
    for k in range(TILES_PER_STEP):
        tile = pl.program_id(0) * TILES_PER_STEP + k
        is_ctx = (tile % n_tiles) >= n_lat_tiles
        _inproj_tile(is_ctx, slice(k * tm, (k + 1) * tm), *refs[k * n_in:(k + 1) * n_in], *shared_and_outs)


def _flat_tile(s, k, n_tiles):
    t = s * TILES_PER_STEP + k
    return t // n_tiles, t % n_tiles


def _source_specs(bt, tm, d, n_lat_tiles, ctx_tile0):
    return [
        pl.BlockSpec((None, tm, d), lambda s: (bt(s)[0], jnp.minimum(bt(s)[1], n_lat_tiles - 1), 0)),
        pl.BlockSpec((None, tm, d), lambda s: (bt(s)[0], ctx_tile0 + jnp.maximum(bt(s)[1] - n_lat_tiles, 0), 0)),
        pl.BlockSpec((None, None, 6, d), lambda s: (bt(s)[0], (bt(s)[1] >= n_lat_tiles).astype(jnp.int32), 0, 0))]


def _inproj(x_src, c_src, ctx_tile0, msel, norm_g, w_in_p, qg, kg, cos_t, sin_t, n_lat_tiles, n_tiles):
    b, _, d = x_src.shape
    tm = TOKEN_TILE
    n_flat = b * n_tiles
    assert n_flat % TILES_PER_STEP == 0
    widths = [(ATTN_W, _BF16), (2 * KV_W, _BF16), (LRU_W, _BF16), (LRU_W, _F32), (S5_W, _F32)]
    in_specs, in_args = [], []
    for k in range(TILES_PER_STEP):
        bt = functools.partial(_flat_tile, k=k, n_tiles=n_tiles)
        in_specs += [*_source_specs(bt, tm, d, n_lat_tiles, ctx_tile0),
                     pl.BlockSpec((tm, LANES), lambda s, bt=bt: (bt(s)[1], 0)),
                     pl.BlockSpec((tm, LANES), lambda s, bt=bt: (bt(s)[1], 0))]
        in_args += [x_src, c_src, msel, cos_t, sin_t]
    in_specs += [_const_spec((1, d)), _const_spec((d, IN_W)), _const_spec((1, LANES)), _const_spec((1, LANES))]
    res = pl.pallas_call(
        functools.partial(_inproj_kernel, n_lat_tiles=n_lat_tiles, n_tiles=n_tiles),
        grid=(n_flat // TILES_PER_STEP,),
        in_specs=in_specs,
        out_specs=[pl.BlockSpec((TILES_PER_STEP * tm, w), lambda s: (s, 0)) for w, _ in widths],
        out_shape=[jax.ShapeDtypeStruct((n_flat * tm, w), dt) for w, dt in widths],
        compiler_params=_cparams(1),
        name="inproj",
    )(*in_args, norm_g, w_in_p, qg, kg)
    return [r.reshape(b, n_tiles * tm, r.shape[-1]) for r in res]


def _attend(q_blk, pieces, sink_ref):
    n_slab = N_KV_HEADS * GQA_GROUP
    lane = _lane_iota((BLOCK, LANES))
    low = lane < HEAD_DIM
    zero = jnp.zeros((BLOCK, LANES), _BF16)
    slabs = [[None] * GQA_GROUP for _ in range(N_KV_HEADS)]
    for g in range(GQA_GROUP):
        blk = q_blk[:, g * LANES:(g + 1) * LANES]
        slabs[0][g] = jnp.where(low, blk, zero)
        slabs[1][g] = jnp.where(low, zero, blk)
    qst = jnp.concatenate(slabs[0] + slabs[1], axis=0)
    half = GQA_GROUP * BLOCK
    m = jnp.concatenate([jnp.full((BLOCK, LANES), sink_ref[s] * LOG2E, _F32) for s in range(n_slab)], axis=0)
    lane_h = _lane_iota((half, LANES)) < HEAD_DIM
    acc = [jnp.where(lane_h, 0.0, 1.0), jnp.where(lane_h, 1.0, 0.0)]
    for t0 in range(0, len(pieces), 2):
        tile = pieces[t0:t0 + 2]
        keys = jnp.concatenate([p[0] for p in tile], axis=0)
        vals = jnp.concatenate([p[1] for p in tile], axis=0)
        s = lax.dot_general(qst, keys, (((1,), (1,)), ((), ())), preferred_element_type=_F32)
        if any(p[2] is not None for p in tile):
            bias = jnp.concatenate([p[2] if p[2] is not None else jnp.zeros((BLOCK, BLOCK), _F32) for p in tile],
                                   axis=1)
            s = jnp.concatenate([s[i * BLOCK:(i + 1) * BLOCK] + bias for i in range(n_slab)], axis=0)
        m_new = jnp.maximum(m, jnp.broadcast_to(jnp.max(s, axis=-1, keepdims=True), m.shape))
        alpha = jnp.exp2(m - m_new)
        p = jnp.exp2(s - jnp.concatenate([m_new] * (s.shape[1] // LANES), axis=1)).astype(_BF16)
        m = m_new
        low_v = _lane_iota(vals.shape) < HEAD_DIM
        one = jnp.ones(vals.shape, _BF16)
        vext = (jnp.where(low_v, vals, one), jnp.where(low_v, one, vals))
        for hk in range(N_KV_HEADS):
            rs = slice(hk * half, (hk + 1) * half)
            acc[hk] = alpha[rs] * acc[hk] + _dot(p[rs], vext[hk])
    outs = []
    for g in range(GQA_GROUP):
        a0 = acc[0][g * BLOCK:(g + 1) * BLOCK]
        a1 = acc[1][g * BLOCK:(g + 1) * BLOCK]
        num = jnp.where(low, a0, a1)
        den = pltpu.roll(jnp.where(low, a1, a0), HEAD_DIM, 1)
        outs.append((num / den).astype(_BF16))
    return outs


def _attn_kernel(sink_ref, q_ref, kvp_ref, kvc_ref, kvn_ref, kvx_ref, o_ref, *, n_lat_pairs):
    qi = pl.program_id(1)
    is_lat = qi < n_lat_pairs
    row = lax.broadcasted_iota(jnp.int32, (BLOCK, BLOCK), 0)
    col = lax.broadcasted_iota(jnp.int32, (BLOCK, BLOCK), 1)
    open_ = jnp.where(is_lat, 0.0, NEG)
    tri_prev = jnp.where(col >= row, open_, NEG)
    tri_next = jnp.where(col <= row, open_, NEG)
    full = jnp.full((BLOCK, BLOCK), open_, _F32)
    shut = jnp.full((BLOCK, BLOCK), NEG, _F32)
    has_prev = is_lat & (qi > 0)
    has_next = qi < n_lat_pairs - 1

    def kv(ref, r0=0):
        return ref[r0:r0 + BLOCK, :KV_W], ref[r0:r0 + BLOCK, KV_W:]

    ctx = [kv(kvx_ref, c * BLOCK) + (None,) for c in range(kvx_ref.shape[0] // BLOCK)]
    lo, hi = kv(kvc_ref, 0), kv(kvc_ref, BLOCK)
    first = [kv(kvp_ref) + (jnp.where(has_prev, tri_prev, shut),), lo + (full,), hi + (tri_next,)] + ctx
    second = [lo + (tri_prev,), hi + (full,), kv(kvn_ref) + (jnp.where(has_next, tri_next, shut),)] + ctx
    for half, pieces in enumerate((first, second)):
        outs = _attend(q_ref[half * BLOCK:(half + 1) * BLOCK, :], pieces, sink_ref)
        for g in range(GQA_GROUP):
            o_ref[half * BLOCK:(half + 1) * BLOCK, g * LANES:(g + 1) * LANES] = outs[g]


def _attention(q, kv, sink, n_lat, n_ctx, with_ctx_queries):
    b, s, _ = q.shape
    pair = 2 * BLOCK
    nlp = n_lat // pair
    nq = s // pair if with_ctx_queries else nlp
    last = s // BLOCK - 1
    halo = lambda f: pl.BlockSpec((None, BLOCK, 2 * KV_W), lambda bi, i: (bi, f(i), 0))
    return pl.pallas_call(
        functools.partial(_attn_kernel, n_lat_pairs=nlp),
        grid=(b, nq),
        in_specs=[pl.BlockSpec(memory_space=pltpu.SMEM),
                  pl.BlockSpec((None, pair, ATTN_W), lambda bi, i: (bi, i, 0)),
                  halo(lambda i: jnp.maximum(2 * i - 1, 0)),
                  pl.BlockSpec((None, pair, 2 * KV_W), lambda bi, i: (bi, i, 0)),
                  halo(lambda i: jnp.minimum(2 * i + 2, last)),
                  pl.BlockSpec((None, n_ctx, 2 * KV_W), lambda bi, i: (bi, n_lat // n_ctx, 0))],
        out_specs=pl.BlockSpec((None, pair, ATTN_W), lambda bi, i: (bi, i, 0)),
        out_shape=jax.ShapeDtypeStruct((b, nq * pair, ATTN_W), _BF16),
        compiler_params=_cparams(2),
        name="attention",
    )(sink, q, kv, kv, kv, kv)


def _shift_rows(x, n_rows_out, shift):
    pad = jnp.zeros((n_rows_out - x.shape[0], x.shape[1]), x.dtype)
    return pltpu.roll(jnp.concatenate([x, pad], axis=0), shift, 0)


def _scan_kernel(xf_ref, xfp_ref, xfn_ref, uf_ref, xb_ref, xbp_ref, xbn_ref, ub_ref,
                 cw_ref, cb_ref, wa_ref, wx_ref, ba_ref, bx_ref, nsp_ref,
                 bd_ref, cm_ref, ar_ref, ai_ref,
                 of_ref, ob_ref,
                 s5_in, s5_out, lru_in, lru_out, s5_state, lru_state, *, n_lat_chunks, n_ctx_chunks):
    i = pl.program_id(0)
    n_b = xf_ref.shape[0]
    t_len = SCAN_CHUNK
    pitch = SCAN_PITCH
    off = pitch - t_len
    t_win = t_len + SUBLANES
    n_slab = 2 * S5_FLAT // (2 * LANES)
    half_w = n_slab * LANES
    n_tot = n_lat_chunks + n_ctx_chunks
    ic = jnp.minimum(i, n_tot - 1)
    chunk_f = jnp.where(ic < n_ctx_chunks, n_lat_chunks + ic, ic - n_ctx_chunks)
    chunk_b = n_tot - 1 - ic

    @pl.when(i == 0)
    def _():
        s5_state[...] = jnp.zeros_like(s5_state)
        lru_state[...] = jnp.zeros_like(lru_state)
        s5_out[...] = jnp.zeros_like(s5_out)
        lru_out[...] = jnp.zeros_like(lru_out)

    def seg_first(c):
        return (c == 0) | (c == n_lat_chunks)

    def seg_last(c):
        return (c == n_lat_chunks - 1) | (c == n_tot - 1)

    def even_rows(b):
        return pl.ds(2 * b * pitch, t_len)

    def odd_rows(b):
        return pl.ds((2 * b + 1) * pitch - off, t_win)

    for d, o_ref in ((0, of_ref), (1, ob_ref)):
        h_even = jnp.concatenate(
            [jnp.concatenate([s5_out[d, sl, even_rows(b), :] for sl in range(n_slab)], axis=1)
             for b in range(n_b)], axis=0)
        h_odd = jnp.concatenate(
            [jnp.concatenate([s5_out[d, sl, odd_rows(b), :] for sl in range(n_slab)], axis=1)
             for b in range(n_b)], axis=0)
        y_even = _dot(h_even.astype(_BF16), cm_ref[d, :half_w, :])
        y_odd = _dot(h_odd.astype(_BF16), cm_ref[d, half_w:, :])
        for b in range(n_b):
            y = (y_even[b * t_len:(b + 1) * t_len]
                 + pltpu.roll(y_odd[b * t_win:(b + 1) * t_win], t_win - off, 0)[:t_len])
            o_ref[b] = jnp.concatenate(
                [lru_out[d, even_rows(b), :],
                 pltpu.roll(lru_out[d, odd_rows(b), :], t_win - off, 0)[:t_len], y], axis=1).astype(_BF16)

    dirs = ((0, xf_ref, xfp_ref, xfn_ref, uf_ref, chunk_f), (1, xb_ref, xbp_ref, xbn_ref, ub_ref, chunk_b))
    for d, x_ref, xp_ref, xn_ref, u_ref, chunk in dirs:
        keep_prev = jnp.where(seg_first(chunk), 0.0, 1.0)
        keep_next = jnp.where(seg_last(chunk), 0.0, 1.0)
        n_e = t_len + 2 * SUBLANES
        xcs = []
        for b in range(n_b):
            xe = jnp.concatenate([x_ref[b], xn_ref[b] * keep_next, xp_ref[b] * keep_prev], axis=0)
            acc = cb_ref[...] + cw_ref[1:2, :] * x_ref[b]
            for j, shift in ((0, 1), (2, n_e - 1), (3, n_e - 2)):
                acc = acc + cw_ref[j:j + 1, :] * pltpu.roll(xe, shift, 0)[:t_len]
            xcs.append(acc)
        xc = jnp.concatenate(xcs, axis=0)
        xcb = xc.astype(_BF16)
        r = jax.nn.sigmoid(_dot(xcb, wa_ref[d]) + ba_ref[d])
        gi = jax.nn.sigmoid(_dot(xcb, wx_ref[d]) + bx_ref[d])
        log_a = nsp_ref[d] * r
        a = jnp.exp(log_a)
        bco = jnp.sqrt(-jnp.tanh(log_a) * (a * a + 1.0)) * gi * xc
        for b in range(n_b):
            rb = slice(b * t_len, (b + 1) * t_len)
            lru_in[d, 0, even_rows(b), :] = a[rb, :LANES]
            lru_in[d, 1, even_rows(b), :] = bco[rb, :LANES]
            lru_in[d, 0, odd_rows(b), :] = _shift_rows(a[rb, LANES:], t_win, off)
            lru_in[d, 1, odd_rows(b), :] = _shift_rows(bco[rb, LANES:], t_win, off)
        u_even = jnp.concatenate([u_ref[b] for b in range(n_b)], axis=0)
        u_odd = jnp.concatenate([_shift_rows(u_ref[b], t_win, off) for b in range(n_b)], axis=0)
        drive_even = _dot(u_even.astype(_BF16), bd_ref[d, :, :half_w])
        drive_odd = _dot(u_odd.astype(_BF16), bd_ref[d, :, half_w:])
        for b in range(n_b):
            for sl in range(n_slab):
                cs = slice(sl * LANES, (sl + 1) * LANES)
                s5_in[d, sl, even_rows(b), :] = drive_even[b * t_len:(b + 1) * t_len, cs]
                s5_in[d, sl, odd_rows(b), :] = drive_odd[b * t_win:(b + 1) * t_win, cs]

    n_k = n_slab // 2
    coef = [[(ar_ref[d, :, k * LANES:(k + 1) * LANES], ai_ref[d, :, k * LANES:(k + 1) * LANES])
             for k in range(n_k)] for d in range(N_DIR)]

    def step(it, carry):
        new = []
        for d in range(N_DIR):
            t = it if d == 0 else t_len - 1 - it
            rows = pl.ds(t, SUBLANES, stride=pitch)
            hs, hl = carry[d]
            nhs = [None] * n_slab
            for k in range(n_k):
                ar, ai = coef[d][k]
                hr, hi = hs[k], hs[n_k + k]
                nr = ar * hr - ai * hi + s5_in[d, k, rows, :]
                ni = ar * hi + ai * hr + s5_in[d, n_k + k, rows, :]
                s5_out[d, k, rows, :] = nr
                s5_out[d, n_k + k, rows, :] = ni
                nhs[k], nhs[n_k + k] = nr, ni
            nhl = lru_in[d, 0, rows, :] * hl + lru_in[d, 1, rows, :]
            lru_out[d, rows, :] = nhl
            new.append((tuple(nhs), nhl))
        return tuple(new)

    @pl.when(i < n_tot)
    def _():
        init = tuple((tuple(s5_state[d, sl] for sl in range(n_slab)), lru_state[d]) for d in range(N_DIR))
        fin = lax.fori_loop(0, t_len, step, init, unroll=4)
        for d in range(N_DIR):
            for sl in range(n_slab):
                s5_state[d, sl] = fin[d][0][sl]
            lru_state[d] = fin[d][1]


def _scans(xl, u, sp, n_lat, n_ctx):
    b, s, _ = xl.shape
    t = SCAN_CHUNK
    nl, nc = n_lat // t, n_ctx // t
    n_tot = nl + nc
    n8 = s // SUBLANES
    per8 = t // SUBLANES
    cf = lambda i: jnp.where(i < nc, nl + i, i - nc)
    cb = lambda i: n_tot - 1 - i
    cur = lambda i: jnp.minimum(i, n_tot - 1)
    done = lambda i: jnp.maximum(i - 1, 0)
    main = lambda f: pl.BlockSpec((b, t, LRU_W), lambda i: (0, f(cur(i)), 0))
    outb = lambda f: pl.BlockSpec((b, t, LRU_W + S5_W), lambda i: (0, f(done(i)), 0))
    prev = lambda f: pl.BlockSpec((b, SUBLANES, LRU_W), lambda i: (0, jnp.maximum(f(cur(i)) * per8 - 1, 0), 0))
    nxt = lambda f: pl.BlockSpec((b, SUBLANES, LRU_W),
                                 lambda i: (0, jnp.minimum((f(cur(i)) + 1) * per8, n8 - 1), 0))
    n_slab = 2 * S5_FLAT // (2 * LANES)
    n_rows = SUBLANES * SCAN_PITCH
    weights = [sp["conv_w"], sp["conv_b"], sp["wa"], sp["wx"], sp["ba"], sp["bx"], sp["nsp"],
               sp["bd"], sp["cm"], sp["ar"], sp["ai"]]
    out_sd = jax.ShapeDtypeStruct((b, s, LRU_W + S5_W), _BF16)
    return pl.pallas_call(
        functools.partial(_scan_kernel, n_lat_chunks=nl, n_ctx_chunks=nc),
        grid=(n_tot + 1,),
        in_specs=[main(cf), prev(cf), nxt(cf), main(cf), main(cb), prev(cb), nxt(cb), main(cb)]
                 + [_const_spec(w.shape) for w in weights],
        out_specs=[outb(cf), outb(cb)],
        out_shape=[out_sd] * 2,
        scratch_shapes=[pltpu.VMEM((N_DIR, n_slab, n_rows, LANES), _F32),
                        pltpu.VMEM((N_DIR, n_slab, n_rows, LANES), _F32),
                        pltpu.VMEM((N_DIR, 2, n_rows, LANES), _F32),
                        pltpu.VMEM((N_DIR, n_rows, LANES), _F32),
                        pltpu.VMEM((N_DIR, n_slab, SUBLANES, LANES), _F32),
                        pltpu.VMEM((N_DIR, SUBLANES, LANES), _F32)],
        compiler_params=_cparams(1),
        name="scans",
    )(xl, xl, xl, u, xl, xl, xl, u, *weights)


def _merge_tile(is_ctx, rows, x_ref, c_ref, mod_ref, attn_ref, gate_ref, of_ref, ob_ref, u_ref,
                mg_ref, dskip_ref, wglu_ref, wout_ref, o_ref):
    x = jnp.where(is_ctx, c_ref[...], x_ref[...])
    both = of_ref[...].astype(_F32) + ob_ref[...].astype(_F32)
    o_lru = jax.nn.gelu(gate_ref[...].astype(_F32)) * both[:, :LRU_W]
    y = u_ref[...] * dskip_ref[...] + both[:, LRU_W:]
    z = jax.nn.gelu(y)
    o_s5 = z * jax.nn.sigmoid(_dot(z.astype(_BF16), wglu_ref[...]))
    mg = mg_ref[...]
    parts = [_rms_rows(attn_ref[...].astype(_F32)) * mg[:, :ATTN_W],
             _rms_rows(o_lru) * mg[:, ATTN_W:ATTN_W + LRU_W],
             _rms_rows(o_s5) * mg[:, ATTN_W + LRU_W:]]
    cat = jnp.concatenate([p.astype(_BF16) for p in parts], axis=1)
    o_ref[rows, :] = x + mod_ref[2:3, :] * _dot(cat, wout_ref[...])


def _merge_kernel(*refs, n_lat_tiles, n_tiles):
    n_in = 8
    shared_and_out = refs[TILES_PER_STEP * n_in:]
    tm = refs[0].shape[0]
    for k in range(TILES_PER_STEP):
        tile = pl.program_id(0) * TILES_PER_STEP + k
        is_ctx = (tile % n_tiles) >= n_lat_tiles
        _merge_tile(is_ctx, slice(k * tm, (k + 1) * tm), *refs[k * n_in:(k + 1) * n_in], *shared_and_out)


def _merge(x_src, c_src, ctx_tile0, msel, attn, gate, of, ob, u, mg_p, dskip, w_glu, w_out_p,
           n_lat_tiles, n_tiles):
    b, _, d = x_src.shape
    tm = TOKEN_TILE
    n_flat = b * n_tiles
    assert n_flat % TILES_PER_STEP == 0
    in_specs, in_args = [], []
    for k in range(TILES_PER_STEP):
        bt = functools.partial(_flat_tile, k=k, n_tiles=n_tiles)
        tok = lambda w, bt=bt: pl.BlockSpec((None, tm, w), lambda s, bt=bt: (*bt(s), 0))
        in_specs += [*_source_specs(bt, tm, d, n_lat_tiles, ctx_tile0),
                     tok(ATTN_W), tok(LRU_W), tok(LRU_W + S5_W), tok(LRU_W + S5_W), tok(S5_W)]
        in_args += [x_src, c_src, msel, attn, gate, of, ob, u]
    in_specs += [_const_spec((1, d)), _const_spec((1, S5_W)), _const_spec((S5_W, S5_W)), _const_spec((d, d))]
    out = pl.pallas_call(
        functools.partial(_merge_kernel, n_lat_tiles=n_lat_tiles, n_tiles=n_tiles),
        grid=(n_flat // TILES_PER_STEP,),
        in_specs=in_specs,
        out_specs=pl.BlockSpec((TILES_PER_STEP * tm, d), lambda s: (s, 0)),
        out_shape=jax.ShapeDtypeStruct((n_flat * tm, d), _F32),
        compiler_params=_cparams(1),
        name="merge",
    )(*in_args, mg_p, dskip, w_glu, w_out_p)
    return out.reshape(b, n_tiles * tm, d)


def _ffn_tile(i, rows, x_ref, xp_ref, xn_ref, mod_ref, g_ref, wup_ref, cw_ref, cb_ref, wd_ref, o_ref, act_ref,
              n_lat_tiles, n_tiles):
    tm = x_ref.shape[0]
    n_e = tm + 2 * SUBLANES
    first = (i == 0) | (i == n_lat_tiles)
    last = (i == n_lat_tiles - 1) | (i == n_tiles - 1)
    x = x_ref[...]
    xe = jnp.concatenate([x, xn_ref[...], xp_ref[...]], axis=0)
    scale = g_ref[...] * (1.0 + mod_ref[4:5, :])
    rowe = lax.broadcasted_iota(jnp.int32, (n_e, 1), 0)
    pad_row = ((rowe >= jnp.where(first, tm + SUBLANES, n_e))
               | ((rowe >= tm) & (rowe < jnp.where(last, tm + SUBLANES, tm))))
    he = jnp.where(pad_row, 0.0, _rms_rows(xe) * scale + mod_ref[3:4, :]).astype(_BF16)
    hc = he[:tm]
    for f in range(D_FF // FF_CHUNK):
        cs = slice(f * FF_CHUNK, (f + 1) * FF_CHUNK)
        ge = _dot(he, wup_ref[:, cs])
        gv = _dot(hc, wup_ref[:, D_FF + f * FF_CHUNK:D_FF + (f + 1) * FF_CHUNK])
        conv = (cb_ref[:, cs] + cw_ref[1:2, cs] * ge[:tm]
                + cw_ref[0:1, cs] * pltpu.roll(ge, 1, 0)[:tm]
                + cw_ref[2:3, cs] * pltpu.roll(ge, n_e - 1, 0)[:tm])
        act_ref[:, cs] = (jax.nn.gelu(conv) * gv).astype(_BF16)
    o_ref[rows, :] = x + mod_ref[5:6, :] * _dot(act_ref[...], wd_ref[...])


def _ffn_kernel(*refs, n_lat_tiles, n_tiles):
    n_in = 4
    g_ref, wup_ref, cw_ref, cb_ref, wd_ref, o_ref, act_ref = refs[FFN_TILES_PER_STEP * n_in:]
    tm = refs[0].shape[0]
    for k in range(FFN_TILES_PER_STEP):
        tile = pl.program_id(0) * FFN_TILES_PER_STEP + k
        _ffn_tile(tile % n_tiles, slice(k * tm, (k + 1) * tm), *refs[k * n_in:(k + 1) * n_in],
                  g_ref, wup_ref, cw_ref, cb_ref, wd_ref, o_ref, act_ref.at[k], n_lat_tiles, n_tiles)


def _ffn(x1, msel, norm_g, layer, w_up, cw, cb, wd, n_lat_tiles, n_tiles):
    b, s, d = x1.shape
    tm = TOKEN_TILE
    per8 = tm // SUBLANES
    n_flat = b * n_tiles
    assert s == n_tiles * tm and n_flat % FFN_TILES_PER_STEP == 0
    x1f = x1.reshape(b * s, d)
    n8 = b * s // SUBLANES
    resident = lambda a: pl.BlockSpec((None,) + a.shape[1:], lambda *_: (layer,) + (0,) * (a.ndim - 1),
                                      pipeline_mode=pl.Buffered(1))
    in_specs, in_args = [], []
    for k in range(FFN_TILES_PER_STEP):
        ft = lambda st, k=k: st * FFN_TILES_PER_STEP + k
        in_specs += [
            pl.BlockSpec((tm, d), lambda st, ft=ft: (ft(st), 0)),
            pl.BlockSpec((SUBLANES, d), lambda st, ft=ft: (jnp.maximum(ft(st) * per8 - 1, 0), 0)),
            pl.BlockSpec((SUBLANES, d), lambda st, ft=ft: (jnp.minimum((ft(st) + 1) * per8, n8 - 1), 0)),
            pl.BlockSpec((None, None, 6, d),
                         lambda st, ft=ft: (ft(st) // n_tiles, (ft(st) % n_tiles >= n_lat_tiles).astype(jnp.int32),
                                            0, 0))]
        in_args += [x1f, x1f, x1f, msel]
    in_specs += [_const_spec((1, d)), resident(w_up), _const_spec(cw.shape), _const_spec(cb.shape), resident(wd)]
    out = pl.pallas_call(
        functools.partial(_ffn_kernel, n_lat_tiles=n_lat_tiles, n_tiles=n_tiles),
        grid=(n_flat // FFN_TILES_PER_STEP,),
        in_specs=in_specs,
        out_specs=pl.BlockSpec((FFN_TILES_PER_STEP * tm, d), lambda st: (st, 0)),
        out_shape=jax.ShapeDtypeStruct((b * s, d), _F32),
        scratch_shapes=[pltpu.VMEM((FFN_TILES_PER_STEP, tm, D_FF), _BF16)],
        compiler_params=_cparams(1),
        name="ffn",
    )(*in_args, norm_g, w_up, cw, cb, wd)
    return out.reshape(b, s, d)


def _rope_tables(n_lat, n_ctx):
    t = jnp.arange(n_lat)
    row = (t // GRID_W).astype(_F32)
    col = (t % GRID_W).astype(_F32)
    n_freq = HEAD_DIM // 4
    inv = ROPE_BASE ** (-jnp.arange(n_freq, dtype=_F32) / n_freq)
    ang_r = row[:, None] * inv
    ang_c = col[:, None] * inv
    cos_h = jnp.concatenate([jnp.cos(ang_r), jnp.cos(ang_r), jnp.cos(ang_c), jnp.cos(ang_c)], axis=1)
    sin_h = jnp.concatenate([-jnp.sin(ang_r), jnp.sin(ang_r), -jnp.sin(ang_c), jnp.sin(ang_c)], axis=1)
    cos_t = jnp.concatenate([jnp.tile(cos_h, (1, 2)), jnp.ones((n_ctx, LANES), _F32)], axis=0)
    sin_t = jnp.concatenate([jnp.tile(sin_h, (1, 2)), jnp.zeros((n_ctx, LANES), _F32)], axis=0)
    return cos_t, sin_t


def _heads_kv_minor(a, axis):
    shape = a.shape
    a = a.reshape(shape[:axis] + (N_KV_HEADS, GQA_GROUP, HEAD_DIM) + shape[axis + 1:])
    return jnp.swapaxes(a, axis, axis + 1).reshape(shape)


def _block_diag(blocks):
    n, r, c = blocks.shape
    tiled = jnp.tile(blocks.reshape(n * r, c), (1, n))
    row = lax.broadcasted_iota(jnp.int32, (n * r, n * c), 0) // r
    col = lax.broadcasted_iota(jnp.int32, (n * r, n * c), 1) // c
    return jnp.where(row == col, tiled, 0.0)


def _s5_params(lam_re, lam_im, log_step, b_re, b_im, c_re, c_im):
    lr = jnp.minimum(lam_re.astype(_F32), -1e-4)
    li = lam_im.astype(_F32)
    dt = jnp.exp(log_step.astype(_F32))[:, None]
    mag = jnp.exp(lr * dt)
    ab_re = mag * jnp.cos(li * dt)
    ab_im = mag * jnp.sin(li * dt)
    nr = ab_re - 1
    den = lr * lr + li * li
    cr = ((nr * lr + ab_im * li) / den)[..., None]
    ci = ((ab_im * lr - nr * li) / den)[..., None]
    br = b_re.astype(_F32)
    bi = b_im.astype(_F32)
    bb_re = cr * br - ci * bi
    bb_im = cr * bi + ci * br
    half = S5_FLAT // 2
    d_re = _block_diag(jnp.swapaxes(bb_re, 1, 2))
    d_im = _block_diag(jnp.swapaxes(bb_im, 1, 2))
    bd = jnp.concatenate([d_re[:, :half], d_im[:, :half], d_re[:, half:], d_im[:, half:]], axis=1)
    r_re = _block_diag(jnp.swapaxes(c_re.astype(_F32), 1, 2))
    r_im = -_block_diag(jnp.swapaxes(c_im.astype(_F32), 1, 2))
    cm = jnp.concatenate([r_re[:half], r_im[:half], r_re[half:], r_im[half:]], axis=0)
    a_re = jnp.tile(ab_re.reshape(2, half), (SUBLANES // 2, 1))
    a_im = jnp.tile(ab_im.reshape(2, half), (SUBLANES // 2, 1))
    return bd.astype(_BF16), cm.astype(_BF16), a_re, a_im


def _per_layer_dir(fn):
    return jax.vmap(jax.vmap(fn))


def kernel(x, c, ctx, c_ctx, w_mod, b_mod, norm1_g, norm2_g, w_in, q_norm_g, k_norm_g, attn_sink,
           lru_conv_w, lru_conv_b, lru_wa, lru_ba, lru_wx, lru_bx, lru_lambda,
           s5_lam_re, s5_lam_im, s5_log_step, s5_b_re, s5_b_im, s5_c_re, s5_c_im, s5_d, s5_w_glu,
           mix_g, w_out, ffn_w_up, ffn_conv_w, ffn_conv_b, ffn_w_down):
    n_b, n_lat, d = x.shape
    n_ctx = ctx.shape[1]
    depth = w_mod.shape[0]
    assert d == D_MODEL and n_b * 2 == SUBLANES
    assert n_lat % TOKEN_TILE == 0 and n_ctx % TOKEN_TILE == 0 and n_lat % n_ctx == 0
    n_lat_tiles = n_lat // TOKEN_TILE
    n_all_tiles = (n_lat + n_ctx) // TOKEN_TILE

    cvecs = jnp.concatenate([c, c_ctx[None], jnp.zeros((SUBLANES - n_b - 1, d), _F32)], axis=0)
    mod = _modulation(cvecs, w_mod, b_mod).reshape(depth, SUBLANES, 6, d)
    msel = jnp.stack([mod[:, :n_b], jnp.broadcast_to(mod[:, n_b:n_b + 1], (depth, n_b, 6, d))], axis=2)
    cos_t, sin_t = _rope_tables(n_lat, n_ctx)

    qg = jnp.tile(q_norm_g, (1, 2))[:, None] * (HEAD_DIM ** -0.5 * LOG2E)
    kg = jnp.tile(k_norm_g, (1, 2))[:, None]
    nsp = -LRU_C * jax.nn.softplus(-lru_lambda.astype(_F32))
    bd, cm, a_re, a_im = _per_layer_dir(_s5_params)(s5_lam_re, s5_lam_im, s5_log_step, s5_b_re, s5_b_im,
                                                    s5_c_re, s5_c_im)
    wa = _per_layer_dir(_block_diag)(lru_wa).astype(_BF16)
    wx = _per_layer_dir(_block_diag)(lru_wx).astype(_BF16)
    w_glu = s5_w_glu.astype(_BF16)
    w_up = ffn_w_up.astype(_BF16)
    w_down = ffn_w_down.astype(_BF16)

    x_src, c_src, ctx_tile0 = x, ctx, 0
    for l in range(depth):
        last = l == depth - 1
        w_in_p = jnp.concatenate([_heads_kv_minor(w_in[l][:, :ATTN_W], 1), w_in[l][:, ATTN_W:]], axis=1).astype(_BF16)
        w_out_p = jnp.concatenate([_heads_kv_minor(w_out[l][:ATTN_W], 0), w_out[l][ATTN_W:]], axis=0).astype(_BF16)
        mix_g_p = jnp.concatenate([_heads_kv_minor(mix_g[l][:ATTN_W], 0), mix_g[l][ATTN_W:]])[None]
        sp = {
            "conv_w": lru_conv_w[l], "conv_b": lru_conv_b[l][None], "wa": wa[l], "wx": wx[l],
            "ba": lru_ba[l][:, None, :], "bx": lru_bx[l][:, None, :], "nsp": nsp[l][:, None, :],
            "bd": bd[l], "cm": cm[l], "ar": a_re[l], "ai": a_im[l],
        }

        q, kv, gate, xl, u = _inproj(x_src, c_src, ctx_tile0, msel[l], norm1_g[l][None], w_in_p, qg[l], kg[l],
                                     cos_t, sin_t, n_lat_tiles, n_all_tiles)
        attn = _attention(q, kv, attn_sink[l], n_lat, n_ctx, with_ctx_queries=not last)
        of, ob = _scans(xl, u, sp, n_lat, n_ctx)
        n_tiles = n_lat_tiles if last else n_all_tiles
        x1 = _merge(x_src, c_src, ctx_tile0, msel[l], attn, gate, of, ob, u, mix_g_p, s5_d[l][None],
                    w_glu[l], w_out_p, n_lat_tiles, n_tiles)
        x_all = _ffn(x1, msel[l], norm2_g[l][None], l, w_up, ffn_conv_w[l], ffn_conv_b[l][None], w_down,
                     n_lat_tiles, n_tiles)
        x_src, c_src, ctx_tile0 = x_all, x_all, n_lat_tiles
    return x_all
```

```python
import functools

import jax
import jax.numpy as jnp
from jax import lax
from jax.experimental import pallas as pl
from jax.experimental.pallas import tpu as pltpu

D_MODEL = 1024
HEAD_DIM = 64
N_Q_HEADS = 8
N_KV_HEADS = 2
GQA_GROUP = N_Q_HEADS // N_KV_HEADS
ATTN_W = N_Q_HEADS * HEAD_DIM
KV_W = N_KV_HEADS * HEAD_DIM
WINDOW = 128
BLOCK = 128
GRID_W = 64
ROPE_BASE = 10000.0
LRU_W = 256
LRU_HEADS = 4
LRU_HEAD_W = LRU_W // LRU_HEADS
LRU_C = 8.0
S5_W = 256
S5_GROUP = 16
S5_GROUPS = S5_W // S5_GROUP
S5_STATE = 64
S5_FLAT = S5_GROUPS * S5_STATE
N_DIR = 2
IN_W = ATTN_W + 2 * KV_W + 2 * LRU_W + S5_W
D_FF = 2816
NEG = -1e30
EPS = 1e-6
LOG2E = 1.4426950408889634

LANES = 128
SUBLANES = 8
TOKEN_TILE = 256
TILES_PER_STEP = 4
FFN_TILES_PER_STEP = 4
SCAN_CHUNK = 128
SCAN_PITCH = SCAN_CHUNK + SUBLANES // 2
FF_CHUNK = 256
VMEM_LIMIT = 56 * 1024 * 1024

_F32 = jnp.float32
_BF16 = jnp.bfloat16


def _cparams(n_axes):
    return pltpu.CompilerParams(dimension_semantics=("arbitrary",) * n_axes,
                                vmem_limit_bytes=VMEM_LIMIT)


def _const_spec(shape):
    nd = len(shape)
    return pl.BlockSpec(shape, lambda *_: (0,) * nd)


def _dot(a, b):
    return jnp.dot(a, b, preferred_element_type=_F32)


def _lane_iota(shape):
    return lax.broadcasted_iota(jnp.int32, shape, len(shape) - 1)


def _mod_kernel(c_ref, w_ref, b_ref, o_ref):
    a = jax.nn.silu(c_ref[...])
    o_ref[...] = jnp.dot(a, w_ref[...], preferred_element_type=_F32,
                         precision=lax.Precision.HIGHEST) + b_ref[...]


def _modulation(cvecs, w_mod, b_mod):
    depth, d, n = w_mod.shape
    nt = 1536
    return pl.pallas_call(
        _mod_kernel,
        grid=(depth, n // nt),
        in_specs=[pl.BlockSpec((SUBLANES, d), lambda l, j: (0, 0)),
                  pl.BlockSpec((None, d, nt), lambda l, j: (l, 0, j)),
                  pl.BlockSpec((None, 1, nt), lambda l, j: (l, 0, j))],
        out_specs=pl.BlockSpec((None, SUBLANES, nt), lambda l, j: (l, 0, j)),
        out_shape=jax.ShapeDtypeStruct((depth, SUBLANES, n), _F32),
        compiler_params=_cparams(2),
        name="modulation",
    )(cvecs, w_mod, b_mod.reshape(depth, 1, n))


def _rms_rows(x):
    return x * lax.rsqrt(jnp.mean(x * x, axis=-1, keepdims=True) + EPS)


def _head_norm_rope(t, gain, cos, sin):
    lane = _lane_iota(t.shape)
    low = lane < HEAD_DIM
    sq = t * t
    s_low = jnp.sum(jnp.where(low, sq, 0.0), axis=-1, keepdims=True)
    s_all = jnp.sum(sq, axis=-1, keepdims=True)
    ms = jnp.where(low, s_low, s_all - s_low) * (1.0 / HEAD_DIM)
    tn = t * lax.rsqrt(ms + EPS) * gain
    partner = jnp.where((lane % 32) < 16, pltpu.roll(tn, LANES - 16, 1), pltpu.roll(tn, 16, 1))
    return tn * cos + partner * sin


def _inproj_tile(is_ctx, rows, x_ref, c_ref, mod_ref, cos_ref, sin_ref, g_ref, w_ref, qg_ref, kg_ref,
                 q_ref, kv_ref, gate_ref, xl_ref, u_ref):
    x = jnp.where(is_ctx, c_ref[...], x_ref[...])
    scale = g_ref[...] * (1.0 + mod_ref[1:2, :])
    h = _rms_rows(x) * scale + mod_ref[0:1, :]
    proj = _dot(h.astype(_BF16), w_ref[...])
    cos = cos_ref[...]
    sin = sin_ref[...]
    for g in range(GQA_GROUP):
        blk = proj[:, g * LANES:(g + 1) * LANES]
        q_ref[rows, g * LANES:(g + 1) * LANES] = _head_norm_rope(blk, qg_ref[...], cos, sin).astype(_BF16)
    o = ATTN_W
    kv_ref[rows, :KV_W] = _head_norm_rope(proj[:, o:o + KV_W], kg_ref[...], cos, sin).astype(_BF16)
    o += KV_W
    kv_ref[rows, KV_W:] = proj[:, o:o + KV_W].astype(_BF16)
    o += KV_W
    gate_ref[rows, :] = proj[:, o:o + LRU_W].astype(_BF16)
    o += LRU_W
    xl_ref[rows, :] = proj[:, o:o + LRU_W]
    o += LRU_W
    u_ref[rows, :] = proj[:, o:o + S5_W]


def _inproj_kernel(*refs, n_lat_tiles, n_tiles):
    n_in = 5
    shared_and_outs = refs[TILES_PER_STEP * n_in:]
    tm = refs[0].shape[0]
    for k in range(TILES_PER_STEP):
        tile = pl.program_id(0) * TILES_PER_STEP + k
        is_ctx = (tile % n_tiles) >= n_lat_tiles
        _inproj_tile(is_ctx, slice(k * tm, (k + 1) * tm), *refs[k * n_in:(k + 1) * n_in], *shared_and_outs)


def _flat_tile(s, k, n_tiles):
    t = s * TILES_PER_STEP + k
    return t // n_tiles, t % n_tiles


def _source_specs(bt, tm, d, n_lat_tiles, ctx_tile0):
    return [
        pl.BlockSpec((None, tm, d), lambda s: (bt(s)[0], jnp.minimum(bt(s)[1], n_lat_tiles - 1), 0)),
        pl.BlockSpec((None, tm, d), lambda s: (bt(s)[0], ctx_tile0 + jnp.maximum(bt(s)[1] - n_lat_tiles, 0), 0)),
        pl.BlockSpec((None, None, 6, d), lambda s: (bt(s)[0], (bt(s)[1] >= n_lat_tiles).astype(jnp.int32), 0, 0))]


def _inproj(x_src, c_src, ctx_tile0, msel, norm_g, w_in_p, qg, kg, cos_t, sin_t, n_lat_tiles, n_tiles):
    b, _, d = x_src.shape
    tm = TOKEN_TILE
    n_flat = b * n_tiles
    assert n_flat % TILES_PER_STEP == 0
    widths = [(ATTN_W, _BF16), (2 * KV_W, _BF16), (LRU_W, _BF16), (LRU_W, _F32), (S5_W, _F32)]
    in_specs, in_args = [], []
    for k in range(TILES_PER_STEP):
        bt = functools.partial(_flat_tile, k=k, n_tiles=n_tiles)
        in_specs += [*_source_specs(bt, tm, d, n_lat_tiles, ctx_tile0),
                     pl.BlockSpec((tm, LANES), lambda s, bt=bt: (bt(s)[1], 0)),
                     pl.BlockSpec((tm, LANES), lambda s, bt=bt: (bt(s)[1], 0))]
        in_args += [x_src, c_src, msel, cos_t, sin_t]
    in_specs += [_const_spec((1, d)), _const_spec((d, IN_W)), _const_spec((1, LANES)), _const_spec((1, LANES))]
    res = pl.pallas_call(
        functools.partial(_inproj_kernel, n_lat_tiles=n_lat_tiles, n_tiles=n_tiles),
        grid=(n_flat // TILES_PER_STEP,),
        in_specs=in_specs,
        out_specs=[pl.BlockSpec((TILES_PER_STEP * tm, w), lambda s: (s, 0)) for w, _ in widths],
        out_shape=[jax.ShapeDtypeStruct((n_flat * tm, w), dt) for w, dt in widths],
        compiler_params=_cparams(1),
        name="inproj",
    )(*in_args, norm_g, w_in_p, qg, kg)
    return [r.reshape(b, n_tiles * tm, r.shape[-1]) for r in res]


def _attend(q_blk, pieces, sink_ref):
    n_slab = N_KV_HEADS * GQA_GROUP
    lane = _lane_iota((BLOCK, LANES))
    low = lane < HEAD_DIM
    zero = jnp.zeros((BLOCK, LANES), _BF16)
    slabs = [[None] * GQA_GROUP for _ in range(N_KV_HEADS)]
    for g in range(GQA_GROUP):
        blk = q_blk[:, g * LANES:(g + 1) * LANES]
        slabs[0][g] = jnp.where(low, blk, zero)
        slabs[1][g] = jnp.where(low, zero, blk)
    qst = jnp.concatenate(slabs[0] + slabs[1], axis=0)
    half = GQA_GROUP * BLOCK
    m = jnp.concatenate([jnp.full((BLOCK, LANES), sink_ref[s] * LOG2E, _F32) for s in range(n_slab)], axis=0)
    lane_h = _lane_iota((half, LANES)) < HEAD_DIM
    acc = [jnp.where(lane_h, 0.0, 1.0), jnp.where(lane_h, 1.0, 0.0)]
    for t0 in range(0, len(pieces), 2):
        tile = pieces[t0:t0 + 2]
        keys = jnp.concatenate([p[0] for p in tile], axis=0)
        vals = jnp.concatenate([p[1] for p in tile], axis=0)
        s = lax.dot_general(qst, keys, (((1,), (1,)), ((), ())), preferred_element_type=_F32)
        if any(p[2] is not None for p in tile):
            bias = jnp.concatenate([p[2] if p[2] is not None else jnp.zeros((BLOCK, BLOCK), _F32) for p in tile],
                                   axis=1)
            s = jnp.concatenate([s[i * BLOCK:(i + 1) * BLOCK] + bias for i in range(n_slab)], axis=0)
        m_new = jnp.maximum(m, jnp.broadcast_to(jnp.max(s, axis=-1, keepdims=True), m.shape))
        alpha = jnp.exp2(m - m_new)
        p = jnp.exp2(s - jnp.concatenate([m_new] * (s.shape[1] // LANES), axis=1)).astype(_BF16)
        m = m_new
        low_v = _lane_iota(vals.shape) < HEAD_DIM
        one = jnp.ones(vals.shape, _BF16)
        vext = (jnp.where(low_v, vals, one), jnp.where(low_v, one, vals))
        for hk in range(N_KV_HEADS):
            rs = slice(hk * half, (hk + 1) * half)
            acc[hk] = alpha[rs] * acc[hk] + _dot(p[rs], vext[hk])
    outs = []
    for g in range(GQA_GROUP):
        a0 = acc[0][g * BLOCK:(g + 1) * BLOCK]
        a1 = acc[1][g * BLOCK:(g + 1) * BLOCK]
        num = jnp.where(low, a0, a1)
        den = pltpu.roll(jnp.where(low, a1, a0), HEAD_DIM, 1)
        outs.append((num / den).astype(_BF16))
    return outs


def _attn_kernel(sink_ref, q_ref, kvp_ref, kvc_ref, kvn_ref, kvx_ref, o_ref, *, n_lat_pairs):
    qi = pl.program_id(1)
    is_lat = qi < n_lat_pairs
    row = lax.broadcasted_iota(jnp.int32, (BLOCK, BLOCK), 0)
    col = lax.broadcasted_iota(jnp.int32, (BLOCK, BLOCK), 1)
    open_ = jnp.where(is_lat, 0.0, NEG)
    tri_prev = jnp.where(col >= row, open_, NEG)
    tri_next = jnp.where(col <= row, open_, NEG)
    full = jnp.full((BLOCK, BLOCK), open_, _F32)
    shut = jnp.full((BLOCK, BLOCK), NEG, _F32)
    has_prev = is_lat & (qi > 0)
    has_next = qi < n_lat_pairs - 1

    def kv(ref, r0=0):
        return ref[r0:r0 + BLOCK, :KV_W], ref[r0:r0 + BLOCK, KV_W:]

    ctx = [kv(kvx_ref, c * BLOCK) + (None,) for c in range(kvx_ref.shape[0] // BLOCK)]
    lo, hi = kv(kvc_ref, 0), kv(kvc_ref, BLOCK)
    first = [kv(kvp_ref) + (jnp.where(has_prev, tri_prev, shut),), lo + (full,), hi + (tri_next,)] + ctx
    second = [lo + (tri_prev,), hi + (full,), kv(kvn_ref) + (jnp.where(has_next, tri_next, shut),)] + ctx
    for half, pieces in enumerate((first, second)):
        outs = _attend(q_ref[half * BLOCK:(half + 1) * BLOCK, :], pieces, sink_ref)
        for g in range(GQA_GROUP):
            o_ref[half * BLOCK:(half + 1) * BLOCK, g * LANES:(g + 1) * LANES] = outs[g]


def _attention(q, kv, sink, n_lat, n_ctx, with_ctx_queries):
    b, s, _ = q.shape
    pair = 2 * BLOCK
    nlp = n_lat // pair
    nq = s // pair if with_ctx_queries else nlp
    last = s // BLOCK - 1
    halo = lambda f: pl.BlockSpec((None, BLOCK, 2 * KV_W), lambda bi, i: (bi, f(i), 0))
    return pl.pallas_call(
        functools.partial(_attn_kernel, n_lat_pairs=nlp),
        grid=(b, nq),
        in_specs=[pl.BlockSpec(memory_space=pltpu.SMEM),
                  pl.BlockSpec((None, pair, ATTN_W), lambda bi, i: (bi, i, 0)),
                  halo(lambda i: jnp.maximum(2 * i - 1, 0)),
                  pl.BlockSpec((None, pair, 2 * KV_W), lambda bi, i: (bi, i, 0)),
                  halo(lambda i: jnp.minimum(2 * i + 2, last)),
                  pl.BlockSpec((None, n_ctx, 2 * KV_W), lambda bi, i: (bi, n_lat // n_ctx, 0))],
        out_specs=pl.BlockSpec((None, pair, ATTN_W), lambda bi, i: (bi, i, 0)),
        out_shape=jax.ShapeDtypeStruct((b, nq * pair, ATTN_W), _BF16),
        compiler_params=_cparams(2),
        name="attention",
    )(sink, q, kv, kv, kv, kv)


def _shift_rows(x, n_rows_out, shift):
    pad = jnp.zeros((n_rows_out - x.shape[0], x.shape[1]), x.dtype)
    return pltpu.roll(jnp.concatenate([x, pad], axis=0), shift, 0)


def _scan_kernel(xf_ref, xfp_ref, xfn_ref, uf_ref, xb_ref, xbp_ref, xbn_ref, ub_ref,
                 cw_ref, cb_ref, wa_ref, wx_ref, ba_ref, bx_ref, nsp_ref,
                 bd_ref, cm_ref, ar_ref, ai_ref,
                 of_ref, ob_ref,
                 s5_in, s5_out, lru_in, lru_out, s5_state, lru_state, *, n_lat_chunks, n_ctx_chunks):
    i = pl.program_id(0)
    n_b = xf_ref.shape[0]
    t_len = SCAN_CHUNK
    pitch = SCAN_PITCH
    off = pitch - t_len
    t_win = t_len + SUBLANES
    n_slab = 2 * S5_FLAT // (2 * LANES)
    half_w = n_slab * LANES
    n_tot = n_lat_chunks + n_ctx_chunks
    ic = jnp.minimum(i, n_tot - 1)
    chunk_f = jnp.where(ic < n_ctx_chunks, n_lat_chunks + ic, ic - n_ctx_chunks)
    chunk_b = n_tot - 1 - ic

    @pl.when(i == 0)
    def _():
        s5_state[...] = jnp.zeros_like(s5_state)
        lru_state[...] = jnp.zeros_like(lru_state)
        s5_out[...] = jnp.zeros_like(s5_out)
        lru_out[...] = jnp.zeros_like(lru_out)

    def seg_first(c):
        return (c == 0) | (c == n_lat_chunks)

    def seg_last(c):
        return (c == n_lat_chunks - 1) | (c == n_tot - 1)

    def even_rows(b):
        return pl.ds(2 * b * pitch, t_len)

    def odd_rows(b):
        return pl.ds((2 * b + 1) * pitch - off, t_win)

    for d, o_ref in ((0, of_ref), (1, ob_ref)):
        h_even = jnp.concatenate(
            [jnp.concatenate([s5_out[d, sl, even_rows(b), :] for sl in range(n_slab)], axis=1)
             for b in range(n_b)], axis=0)
        h_odd = jnp.concatenate(
            [jnp.concatenate([s5_out[d, sl, odd_rows(b), :] for sl in range(n_slab)], axis=1)
             for b in range(n_b)], axis=0)
        y_even = _dot(h_even.astype(_BF16), cm_ref[d, :half_w, :])
        y_odd = _dot(h_odd.astype(_BF16), cm_ref[d, half_w:, :])
        for b in range(n_b):
            y = (y_even[b * t_len:(b + 1) * t_len]
                 + pltpu.roll(y_odd[b * t_win:(b + 1) * t_win], t_win - off, 0)[:t_len])
            o_ref[b] = jnp.concatenate(
                [lru_out[d, even_rows(b), :],
                 pltpu.roll(lru_out[d, odd_rows(b), :], t_win - off, 0)[:t_len], y], axis=1).astype(_BF16)

    dirs = ((0, xf_ref, xfp_ref, xfn_ref, uf_ref, chunk_f), (1, xb_ref, xbp_ref, xbn_ref, ub_ref, chunk_b))
    for d, x_ref, xp_ref, xn_ref, u_ref, chunk in dirs:
        keep_prev = jnp.where(seg_first(chunk), 0.0, 1.0)
        keep_next = jnp.where(seg_last(chunk), 0.0, 1.0)
        n_e = t_len + 2 * SUBLANES
        xcs = []
        for b in range(n_b):
            xe = jnp.concatenate([x_ref[b], xn_ref[b] * keep_next, xp_ref[b] * keep_prev], axis=0)
            acc = cb_ref[...] + cw_ref[1:2, :] * x_ref[b]
            for j, shift in ((0, 1), (2, n_e - 1), (3, n_e - 2)):
                acc = acc + cw_ref[j:j + 1, :] * pltpu.roll(xe, shift, 0)[:t_len]
            xcs.append(acc)
        xc = jnp.concatenate(xcs, axis=0)
        xcb = xc.astype(_BF16)
        r = jax.nn.sigmoid(_dot(xcb, wa_ref[d]) + ba_ref[d])
        gi = jax.nn.sigmoid(_dot(xcb, wx_ref[d]) + bx_ref[d])
        log_a = nsp_ref[d] * r
        a = jnp.exp(log_a)
        bco = jnp.sqrt(-jnp.tanh(log_a) * (a * a + 1.0)) * gi * xc
        for b in range(n_b):
            rb = slice(b * t_len, (b + 1) * t_len)
            lru_in[d, 0, even_rows(b), :] = a[rb, :LANES]
            lru_in[d, 1, even_rows(b), :] = bco[rb, :LANES]
            lru_in[d, 0, odd_rows(b), :] = _shift_rows(a[rb, LANES:], t_win, off)
            lru_in[d, 1, odd_rows(b), :] = _shift_rows(bco[rb, LANES:], t_win, off)
        u_even = jnp.concatenate([u_ref[b] for b in range(n_b)], axis=0)
        u_odd = jnp.concatenate([_shift_rows(u_ref[b], t_win, off) for b in range(n_b)], axis=0)
        drive_even = _dot(u_even.astype(_BF16), bd_ref[d, :, :half_w])
        drive_odd = _dot(u_odd.astype(_BF16), bd_ref[d, :, half_w:])
        for b in range(n_b):
            for sl in range(n_slab):
                cs = slice(sl * LANES, (sl + 1) * LANES)
                s5_in[d, sl, even_rows(b), :] = drive_even[b * t_len:(b + 1) * t_len, cs]
                s5_in[d, sl, odd_rows(b), :] = drive_odd[b * t_win:(b + 1) * t_win, cs]

    n_k = n_slab // 2
    coef = [[(ar_ref[d, :, k * LANES:(k + 1) * LANES], ai_ref[d, :, k * LANES:(k + 1) * LANES])
             for k in range(n_k)] for d in range(N_DIR)]

    def step(it, carry):
        new = []
        for d in range(N_DIR):
            t = it if d == 0 else t_len - 1 - it
            rows = pl.ds(t, SUBLANES, stride=pitch)
            hs, hl = carry[d]
            nhs = [None] * n_slab
            for k in range(n_k):
                ar, ai = coef[d][k]
                hr, hi = hs[k], hs[n_k + k]
                nr = ar * hr - ai * hi + s5_in[d, k, rows, :]
                ni = ar * hi + ai * hr + s5_in[d, n_k + k, rows, :]
                s5_out[d, k, rows, :] = nr
                s5_out[d, n_k + k, rows, :] = ni
                nhs[k], nhs[n_k + k] = nr, ni
            nhl = lru_in[d, 0, rows, :] * hl + lru_in[d, 1, rows, :]
            lru_out[d, rows, :] = nhl
            new.append((tuple(nhs), nhl))
        return tuple(new)

    @pl.when(i < n_tot)
    def _():
        init = tuple((tuple(s5_state[d, sl] for sl in range(n_slab)), lru_state[d]) for d in range(N_DIR))
        fin = lax.fori_loop(0, t_len, step, init, unroll=4)
        for d in range(N_DIR):
            for sl in range(n_slab):
                s5_state[d, sl] = fin[d][0][sl]
            lru_state[d] = fin[d][1]


def _scans(xl, u, sp, n_lat, n_ctx):
    b, s, _ = xl.shape
    t = SCAN_CHUNK
    nl, nc = n_lat // t, n_ctx // t
    n_tot = nl + nc
    n8 = s // SUBLANES
    per8 = t // SUBLANES
    cf = lambda i: jnp.where(i < nc, nl + i, i - nc)
    cb = lambda i: n_tot - 1 - i
    cur = lambda i: jnp.minimum(i, n_tot - 1)
    done = lambda i: jnp.maximum(i - 1, 0)
    main = lambda f: pl.BlockSpec((b, t, LRU_W), lambda i: (0, f(cur(i)), 0))
    outb = lambda f: pl.BlockSpec((b, t, LRU_W + S5_W), lambda i: (0, f(done(i)), 0))
    prev = lambda f: pl.BlockSpec((b, SUBLANES, LRU_W), lambda i: (0, jnp.maximum(f(cur(i)) * per8 - 1, 0), 0))
    nxt = lambda f: pl.BlockSpec((b, SUBLANES, LRU_W),
                                 lambda i: (0, jnp.minimum((f(cur(i)) + 1) * per8, n8 - 1), 0))
    n_slab = 2 * S5_FLAT // (2 * LANES)
    n_rows = SUBLANES * SCAN_PITCH
    weights = [sp["conv_w"], sp["conv_b"], sp["wa"], sp["wx"], sp["ba"], sp["bx"], sp["nsp"],
               sp["bd"], sp["cm"], sp["ar"], sp["ai"]]
    out_sd = jax.ShapeDtypeStruct((b, s, LRU_W + S5_W), _BF16)
    return pl.pallas_call(
        functools.partial(_scan_kernel, n_lat_chunks=nl, n_ctx_chunks=nc),
        grid=(n_tot + 1,),
        in_specs=[main(cf), prev(cf), nxt(cf), main(cf), main(cb), prev(cb), nxt(cb), main(cb)]
                 + [_const_spec(w.shape) for w in weights],
        out_specs=[outb(cf), outb(cb)],
        out_shape=[out_sd] * 2,
        scratch_shapes=[pltpu.VMEM((N_DIR, n_slab, n_rows, LANES), _F32),
                        pltpu.VMEM((N_DIR, n_slab, n_rows, LANES), _F32),
                        pltpu.VMEM((N_DIR, 2, n_rows, LANES), _F32),
                        pltpu.VMEM((N_DIR, n_rows, LANES), _F32),
                        pltpu.VMEM((N_DIR, n_slab, SUBLANES, LANES), _F32),
                        pltpu.VMEM((N_DIR, SUBLANES, LANES), _F32)],
        compiler_params=_cparams(1),
        name="scans",
    )(xl, xl, xl, u, xl, xl, xl, u, *weights)


def _merge_tile(is_ctx, rows, x_ref, c_ref, mod_ref, attn_ref, gate_ref, of_ref, ob_ref, u_ref,
                mg_ref, dskip_ref, wglu_ref, wout_ref, o_ref):
    x = jnp.where(is_ctx, c_ref[...], x_ref[...])
    both = of_ref[...].astype(_F32) + ob_ref[...].astype(_F32)
    o_lru = jax.nn.gelu(gate_ref[...].astype(_F32)) * both[:, :LRU_W]
    y = u_ref[...] * dskip_ref[...] + both[:, LRU_W:]
    z = jax.nn.gelu(y)
    o_s5 = z * jax.nn.sigmoid(_dot(z.astype(_BF16), wglu_ref[...]))
    mg = mg_ref[...]
    parts = [_rms_rows(attn_ref[...].astype(_F32)) * mg[:, :ATTN_W],
             _rms_rows(o_lru) * mg[:, ATTN_W:ATTN_W + LRU_W],
             _rms_rows(o_s5) * mg[:, ATTN_W + LRU_W:]]
    cat = jnp.concatenate([p.astype(_BF16) for p in parts], axis=1)
    o_ref[rows, :] = x + mod_ref[2:3, :] * _dot(cat, wout_ref[...])


def _merge_kernel(*refs, n_lat_tiles, n_tiles):
    n_in = 8
    shared_and_out = refs[TILES_PER_STEP * n_in:]
    tm = refs[0].shape[0]
    for k in range(TILES_PER_STEP):
        tile = pl.program_id(0) * TILES_PER_STEP + k
        is_ctx = (tile % n_tiles) >= n_lat_tiles
        _merge_tile(is_ctx, slice(k * tm, (k + 1) * tm), *refs[k * n_in:(k + 1) * n_in], *shared_and_out)


def _merge(x_src, c_src, ctx_tile0, msel, attn, gate, of, ob, u, mg_p, dskip, w_glu, w_out_p,
           n_lat_tiles, n_tiles):
    b, _, d = x_src.shape
    tm = TOKEN_TILE
    n_flat = b * n_tiles
    assert n_flat % TILES_PER_STEP == 0
    in_specs, in_args = [], []
    for k in range(TILES_PER_STEP):
        bt = functools.partial(_flat_tile, k=k, n_tiles=n_tiles)
        tok = lambda w, bt=bt: pl.BlockSpec((None, tm, w), lambda s, bt=bt: (*bt(s), 0))
        in_specs += [*_source_specs(bt, tm, d, n_lat_tiles, ctx_tile0),
                     tok(ATTN_W), tok(LRU_W), tok(LRU_W + S5_W), tok(LRU_W + S5_W), tok(S5_W)]
        in_args += [x_src, c_src, msel, attn, gate, of, ob, u]
    in_specs += [_const_spec((1, d)), _const_spec((1, S5_W)), _const_spec((S5_W, S5_W)), _const_spec((d, d))]
    out = pl.pallas_call(
        functools.partial(_merge_kernel, n_lat_tiles=n_lat_tiles, n_tiles=n_tiles),
        grid=(n_flat // TILES_PER_STEP,),
        in_specs=in_specs,
        out_specs=pl.BlockSpec((TILES_PER_STEP * tm, d), lambda s: (s, 0)),
        out_shape=jax.ShapeDtypeStruct((n_flat * tm, d), _F32),
        compiler_params=_cparams(1),
        name="merge",
    )(*in_args, mg_p, dskip, w_glu, w_out_p)
    return out.reshape(b, n_tiles * tm, d)


def _ffn_tile(i, rows, x_ref, xp_ref, xn_ref, mod_ref, g_ref, wup_ref, cw_ref, cb_ref, wd_ref, o_ref, act_ref,
              n_lat_tiles, n_tiles):
    tm = x_ref.shape[0]
    n_e = tm + 2 * SUBLANES
    first = (i == 0) | (i == n_lat_tiles)
    last = (i == n_lat_tiles - 1) | (i == n_tiles - 1)
    x = x_ref[...]
    xe = jnp.concatenate([x, xn_ref[...], xp_ref[...]], axis=0)
    scale = g_ref[...] * (1.0 + mod_ref[4:5, :])
    rowe = lax.broadcasted_iota(jnp.int32, (n_e, 1), 0)
    pad_row = ((rowe >= jnp.where(first, tm + SUBLANES, n_e))
               | ((rowe >= tm) & (rowe < jnp.where(last, tm + SUBLANES, tm))))
    he = jnp.where(pad_row, 0.0, _rms_rows(xe) * scale + mod_ref[3:4, :]).astype(_BF16)
    hc = he[:tm]
    for f in range(D_FF // FF_CHUNK):
        cs = slice(f * FF_CHUNK, (f + 1) * FF_CHUNK)
        ge = _dot(he, wup_ref[:, cs])
        gv = _dot(hc, wup_ref[:, D_FF + f * FF_CHUNK:D_FF + (f + 1) * FF_CHUNK])
        conv = (cb_ref[:, cs] + cw_ref[1:2, cs] * ge[:tm]
                + cw_ref[0:1, cs] * pltpu.roll(ge, 1, 0)[:tm]
                + cw_ref[2:3, cs] * pltpu.roll(ge, n_e - 1, 0)[:tm])
        act_ref[:, cs] = (jax.nn.gelu(conv) * gv).astype(_BF16)
    o_ref[rows, :] = x + mod_ref[5:6, :] * _dot(act_ref[...], wd_ref[...])


def _ffn_kernel(*refs, n_lat_tiles, n_tiles):
    n_in = 4
    g_ref, wup_ref, cw_ref, cb_ref, wd_ref, o_ref, act_ref = refs[FFN_TILES_PER_STEP * n_in:]
    tm = refs[0].shape[0]
    for k in range(FFN_TILES_PER_STEP):
        tile = pl.program_id(0) * FFN_TILES_PER_STEP + k
        _ffn_tile(tile % n_tiles, slice(k * tm, (k + 1) * tm), *refs[k * n_in:(k + 1) * n_in],
                  g_ref, wup_ref, cw_ref, cb_ref, wd_ref, o_ref, act_ref.at[k], n_lat_tiles, n_tiles)


def _ffn(x1, msel, norm_g, layer, w_up, cw, cb, wd, n_lat_tiles, n_tiles):
    b, s, d = x1.shape
    tm = TOKEN_TILE
    per8 = tm // SUBLANES
    n_flat = b * n_tiles
    assert s == n_tiles * tm and n_flat % FFN_TILES_PER_STEP == 0
    x1f = x1.reshape(b * s, d)
    n8 = b * s // SUBLANES
    resident = lambda a: pl.BlockSpec((None,) + a.shape[1:], lambda *_: (layer,) + (0,) * (a.ndim - 1),
                                      pipeline_mode=pl.Buffered(1))
    in_specs, in_args = [], []
    for k in range(FFN_TILES_PER_STEP):
        ft = lambda st, k=k: st * FFN_TILES_PER_STEP + k
        in_specs += [
            pl.BlockSpec((tm, d), lambda st, ft=ft: (ft(st), 0)),
            pl.BlockSpec((SUBLANES, d), lambda st, ft=ft: (jnp.maximum(ft(st) * per8 - 1, 0), 0)),
            pl.BlockSpec((SUBLANES, d), lambda st, ft=ft: (jnp.minimum((ft(st) + 1) * per8, n8 - 1), 0)),
            pl.BlockSpec((None, None, 6, d),
                         lambda st, ft=ft: (ft(st) // n_tiles, (ft(st) % n_tiles >= n_lat_tiles).astype(jnp.int32),
                                            0, 0))]
        in_args += [x1f, x1f, x1f, msel]
    in_specs += [_const_spec((1, d)), resident(w_up), _const_spec(cw.shape), _const_spec(cb.shape), resident(wd)]
    out = pl.pallas_call(
        functools.partial(_ffn_kernel, n_lat_tiles=n_lat_tiles, n_tiles=n_tiles),
        grid=(n_flat // FFN_TILES_PER_STEP,),
        in_specs=in_specs,
        out_specs=pl.BlockSpec((FFN_TILES_PER_STEP * tm, d), lambda st: (st, 0)),
        out_shape=jax.ShapeDtypeStruct((b * s, d), _F32),
        scratch_shapes=[pltpu.VMEM((FFN_TILES_PER_STEP, tm, D_FF), _BF16)],
        compiler_params=_cparams(1),
        name="ffn",
    )(*in_args, norm_g, w_up, cw, cb, wd)
    return out.reshape(b, s, d)


def _rope_tables(n_lat, n_ctx):
    t = jnp.arange(n_lat)
    row = (t // GRID_W).astype(_F32)
    col = (t % GRID_W).astype(_F32)
    n_freq = HEAD_DIM // 4
    inv = ROPE_BASE ** (-jnp.arange(n_freq, dtype=_F32) / n_freq)
    ang_r = row[:, None] * inv
    ang_c = col[:, None] * inv
    cos_h = jnp.concatenate([jnp.cos(ang_r), jnp.cos(ang_r), jnp.cos(ang_c), jnp.cos(ang_c)], axis=1)
    sin_h = jnp.concatenate([-jnp.sin(ang_r), jnp.sin(ang_r), -jnp.sin(ang_c), jnp.sin(ang_c)], axis=1)
    cos_t = jnp.concatenate([jnp.tile(cos_h, (1, 2)), jnp.ones((n_ctx, LANES), _F32)], axis=0)
    sin_t = jnp.concatenate([jnp.tile(sin_h, (1, 2)), jnp.zeros((n_ctx, LANES), _F32)], axis=0)
    return cos_t, sin_t


def _heads_kv_minor(a, axis):
    shape = a.shape
    a = a.reshape(shape[:axis] + (N_KV_HEADS, GQA_GROUP, HEAD_DIM) + shape[axis + 1:])
    return jnp.swapaxes(a, axis, axis + 1).reshape(shape)


def _block_diag(blocks):
    n, r, c = blocks.shape
    tiled = jnp.tile(blocks.reshape(n * r, c), (1, n))
    row = lax.broadcasted_iota(jnp.int32, (n * r, n * c), 0) // r
    col = lax.broadcasted_iota(jnp.int32, (n * r, n * c), 1) // c
    return jnp.where(row == col, tiled, 0.0)


def _s5_params(lam_re, lam_im, log_step, b_re, b_im, c_re, c_im):
    lr = jnp.minimum(lam_re.astype(_F32), -1e-4)
    li = lam_im.astype(_F32)
    dt = jnp.exp(log_step.astype(_F32))[:, None]
    mag = jnp.exp(lr * dt)
    ab_re = mag * jnp.cos(li * dt)
    ab_im = mag * jnp.sin(li * dt)
    nr = ab_re - 1
    den = lr * lr + li * li
    cr = ((nr * lr + ab_im * li) / den)[..., None]
    ci = ((ab_im * lr - nr * li) / den)[..., None]
    br = b_re.astype(_F32)
    bi = b_im.astype(_F32)
    bb_re = cr * br - ci * bi
    bb_im = cr * bi + ci * br
    half = S5_FLAT // 2
    d_re = _block_diag(jnp.swapaxes(bb_re, 1, 2))
    d_im = _block_diag(jnp.swapaxes(bb_im, 1, 2))
    bd = jnp.concatenate([d_re[:, :half], d_im[:, :half], d_re[:, half:], d_im[:, half:]], axis=1)
    r_re = _block_diag(jnp.swapaxes(c_re.astype(_F32), 1, 2))
    r_im = -_block_diag(jnp.swapaxes(c_im.astype(_F32), 1, 2))
    cm = jnp.concatenate([r_re[:half], r_im[:half], r_re[half:], r_im[half:]], axis=0)
    a_re = jnp.tile(ab_re.reshape(2, half), (SUBLANES // 2, 1))
    a_im = jnp.tile(ab_im.reshape(2, half), (SUBLANES // 2, 1))
    return bd.astype(_BF16), cm.astype(_BF16), a_re, a_im


def _per_layer_dir(fn):
    return jax.vmap(jax.vmap(fn))


def kernel(x, c, ctx, c_ctx, w_mod, b_mod, norm1_g, norm2_g, w_in, q_norm_g, k_norm_g, attn_sink,
           lru_conv_w, lru_conv_b, lru_wa, lru_ba, lru_wx, lru_bx, lru_lambda,
           s5_lam_re, s5_lam_im, s5_log_step, s5_b_re, s5_b_im, s5_c_re, s5_c_im, s5_d, s5_w_glu,
           mix_g, w_out, ffn_w_up, ffn_conv_w, ffn_conv_b, ffn_w_down):
    n_b, n_lat, d = x.shape
    n_ctx = ctx.shape[1]
    depth = w_mod.shape[0]
    assert d == D_MODEL and n_b * 2 == SUBLANES
    assert n_lat % TOKEN_TILE == 0 and n_ctx % TOKEN_TILE == 0 and n_lat % n_ctx == 0
    n_lat_tiles = n_lat // TOKEN_TILE
    n_all_tiles = (n_lat + n_ctx) // TOKEN_TILE

    cvecs = jnp.concatenate([c, c_ctx[None], jnp.zeros((SUBLANES - n_b - 1, d), _F32)], axis=0)
    mod = _modulation(cvecs, w_mod, b_mod).reshape(depth, SUBLANES, 6, d)
    msel = jnp.stack([mod[:, :n_b], jnp.broadcast_to(mod[:, n_b:n_b + 1], (depth, n_b, 6, d))], axis=2)
    cos_t, sin_t = _rope_tables(n_lat, n_ctx)

    qg = jnp.tile(q_norm_g, (1, 2))[:, None] * (HEAD_DIM ** -0.5 * LOG2E)
    kg = jnp.tile(k_norm_g, (1, 2))[:, None]
    nsp = -LRU_C * jax.nn.softplus(-lru_lambda.astype(_F32))
    bd, cm, a_re, a_im = _per_layer_dir(_s5_params)(s5_lam_re, s5_lam_im, s5_log_step, s5_b_re, s5_b_im,
                                                    s5_c_re, s5_c_im)
    wa = _per_layer_dir(_block_diag)(lru_wa).astype(_BF16)
    wx = _per_layer_dir(_block_diag)(lru_wx).astype(_BF16)
    w_glu = s5_w_glu.astype(_BF16)
    w_up = ffn_w_up.astype(_BF16)
    w_down = ffn_w_down.astype(_BF16)

    x_src, c_src, ctx_tile0 = x, ctx, 0
    for l in range(depth):
        last = l == depth - 1
        w_in_p = jnp.concatenate([_heads_kv_minor(w_in[l][:, :ATTN_W], 1), w_in[l][:, ATTN_W:]], axis=1).astype(_BF16)
        w_out_p = jnp.concatenate([_heads_kv_minor(w_out[l][:ATTN_W], 0), w_out[l][ATTN_W:]], axis=0).astype(_BF16)
        mix_g_p = jnp.concatenate([_heads_kv_minor(mix_g[l][:ATTN_W], 0), mix_g[l][ATTN_W:]])[None]
        sp = {
            "conv_w": lru_conv_w[l], "conv_b": lru_conv_b[l][None], "wa": wa[l], "wx": wx[l],
            "ba": lru_ba[l][:, None, :], "bx": lru_bx[l][:, None, :], "nsp": nsp[l][:, None, :],
            "bd": bd[l], "cm": cm[l], "ar": a_re[l], "ai": a_im[l],
        }

        q, kv, gate, xl, u = _inproj(x_src, c_src, ctx_tile0, msel[l], norm1_g[l][None], w_in_p, qg[l], kg[l],
                                     cos_t, sin_t, n_lat_tiles, n_all_tiles)
        attn = _attention(q, kv, attn_sink[l], n_lat, n_ctx, with_ctx_queries=not last)
        of, ob = _scans(xl, u, sp, n_lat, n_ctx)
        n_tiles = n_lat_tiles if last else n_all_tiles
        x1 = _merge(x_src, c_src, ctx_tile0, msel[l], attn, gate, of, ob, u, mix_g_p, s5_d[l][None],
                    w_glu[l], w_out_p, n_lat_tiles, n_tiles)
        x_all = _ffn(x1, msel[l], norm2_g[l][None], l, w_up, ffn_conv_w[l], ffn_conv_b[l][None], w_down,
                     n_lat_tiles, n_tiles)
        x_src, c_src, ctx_tile0 = x_all, x_all, n_lat_tiles
    return x_all
```
